```python
import math
import jax, jax.numpy as jnp
from jax import lax
import numpy as np

D_MODEL = 1024
BATCH = 4
SEQ = 4096
DEPTH = 4
DEC_BATCH = 128
DEC_SEQ = 4
PAST_LEN = 8192
PAGE_SIZE = 128

N_MIXERS = 3
N_A = (DEPTH + 2) // 3
N_B = (DEPTH + 1) // 3
N_C = DEPTH // 3

DIFF_HEADS = 8
DIFF_KV_HEADS = 2
DIFF_REP = DIFF_HEADS // DIFF_KV_HEADS
DIFF_HD = D_MODEL // DIFF_HEADS // 2
DIFF_SUBLN_EPS = 1e-5

MLA_HEADS = 16
MLA_Q_RANK = 256
MLA_KV_RANK = 128
MLA_NOPE = 64
MLA_ROPE = 32
MLA_V = 64
ROPE_THETA = 10000.0

SWA_HEADS = 16
SWA_KV_HEADS = 2
SWA_REP = SWA_HEADS // SWA_KV_HEADS
SWA_HD = 64
WINDOW = 128

D_FF = 2816
CONV_W = 3

Q_BLOCK = 128
EPS = 1e-6
NEG_INF = -1e30

kernel_name = "hybrid_diff_mla_swa_convffn_step"


def rms_norm(x, g, eps=EPS):
    xf = x.astype(jnp.float32)
    y = xf * lax.rsqrt(jnp.mean(xf * xf, axis=-1, keepdims=True) + eps)
    return (y * g.astype(jnp.float32)).astype(x.dtype)


def alibi_slopes(n):
    return jnp.asarray(np.array([2.0 ** (-8.0 * (h + 1) / n) for h in range(n)], dtype=np.float32))


def rope(x, pos):
    d = x.shape[-1]
    half = d // 2
    freqs = ROPE_THETA ** (-jnp.arange(half, dtype=jnp.float32) * 2.0 / d)
    ang = pos.astype(jnp.float32)[:, None] * freqs[None, :]
    shape = (pos.shape[0],) + (1,) * (x.ndim - 3) + (half,)
    cos = jnp.cos(ang).reshape(shape)
    sin = jnp.sin(ang).reshape(shape)
    xf = x.astype(jnp.float32).reshape(x.shape[:-1] + (half, 2))
    x1, x2 = xf[..., 0], xf[..., 1]
    out = jnp.stack([x1 * cos - x2 * sin, x1 * sin + x2 * cos], axis=-1)
    return out.reshape(x.shape).astype(x.dtype)


def sweep_query_blocks(fn, qs, q_pos):
    B, T = qs[0].shape[:2]
    nb = T // Q_BLOCK
    qb = tuple(jnp.moveaxis(q.reshape((B, nb, Q_BLOCK) + q.shape[2:]), 1, 0) for q in qs)
    pb = q_pos.reshape(nb, Q_BLOCK)
    out = lax.map(lambda a: fn(a[0], a[1]), (qb, pb))
    out = jnp.moveaxis(out, 0, 1)
    return out.reshape((B, T) + out.shape[3:])


def diff_attend(q, k, v, q_pos, k_pos, lam, slopes):
    s = jnp.einsum('btgrmd,bsgmd->bgrmts', q, k).astype(jnp.float32) * (DIFF_HD ** -0.5)
    dist = (q_pos[:, None] - k_pos[None, :]).astype(jnp.float32)
    bias = -slopes.reshape(DIFF_KV_HEADS, DIFF_REP)[:, :, None, None, None] * dist
    s = jnp.where(dist >= 0, s + bias, NEG_INF)
    p = jax.nn.softmax(s, axis=-1)
    a = p[:, :, :, 0] - lam * p[:, :, :, 1]
    return jnp.einsum('bgrts,bsge->btgre', a.astype(v.dtype), v)


def diff_attention(xn_p, xn_s, past_k, past_v, past_len, wq, wk, wv, lq1, lk1, lq2, lk2, subln, wo, lam_init):
    slopes = alibi_slopes(DIFF_HEADS)
    f32 = jnp.float32
    lam = (jnp.exp(jnp.sum(lq1.astype(f32) * lk1.astype(f32)))
           - jnp.exp(jnp.sum(lq2.astype(f32) * lk2.astype(f32))) + lam_init)

    def project(xn):
        B, T, _ = xn.shape
        q = (xn @ wq).reshape(B, T, DIFF_KV_HEADS, DIFF_REP, 2, DIFF_HD)
        k = (xn @ wk).reshape(B, T, DIFF_KV_HEADS, 2, DIFF_HD)
        v = (xn @ wv).reshape(B, T, DIFF_KV_HEADS, 2 * DIFF_HD)
        return q, k, v

    def finish(o):
        B, T = o.shape[:2]
        o = rms_norm(o, subln, DIFF_SUBLN_EPS) * (1.0 - lam_init)
        return o.reshape(B, T, DIFF_HEADS * 2 * DIFF_HD) @ wo

    q_p, k_p, v_p = project(xn_p)
    pos_p = jnp.arange(xn_p.shape[1])
    o_p = sweep_query_blocks(lambda qs, pb: diff_attend(qs[0], k_p, v_p, pb, pos_p, lam, slopes), (q_p,), pos_p)

    q_s, k_s, v_s = project(xn_s)
    t_s = xn_s.shape[1]
    pos_s = past_len + jnp.arange(t_s)
    k_all = jnp.concatenate([past_k, k_s], axis=1)
    v_all = jnp.concatenate([past_v, v_s], axis=1)
    o_s = diff_attend(q_s, k_all, v_all, pos_s, jnp.arange(past_len + t_s), lam, slopes)
    return finish(o_p), finish(o_s), k_p, v_p, k_s, v_s


def mla_attend(q_lat, q_pe, ckv, kpe, q_pos, k_pos):
    s = (jnp.einsum('bthr,bsr->bhts', q_lat, ckv)
         + jnp.einsum('bthp,bsp->bhts', q_pe, kpe)).astype(jnp.float32) * ((MLA_NOPE + MLA_ROPE) ** -0.5)
    s = jnp.where(q_pos[:, None] >= k_pos[None, :], s, NEG_INF)
    p = jax.nn.softmax(s, axis=-1)
    return jnp.einsum('bhts,bsr->bthr', p.astype(ckv.dtype), ckv)


def mla_attention(xn_p, xn_s, past_ckv, past_kpe, past_len, wdq, q_norm, wuq, wdkv, kv_norm, wuk, wuv, wo):
    def project(xn, pos):
        B, T, _ = xn.shape
        cq = rms_norm(xn @ wdq, q_norm)
        q = (cq @ wuq).reshape(B, T, MLA_HEADS, MLA_NOPE + MLA_ROPE)
        q_pe = rope(q[..., MLA_NOPE:], pos)
        q_lat = jnp.einsum('bthn,rhn->bthr', q[..., :MLA_NOPE], wuk)
        kv = xn @ wdkv
        ckv = rms_norm(kv[..., :MLA_KV_RANK], kv_norm)
        kpe = rope(kv[..., MLA_KV_RANK:], pos)
        return q_lat, q_pe, ckv, kpe

    def finish(o_lat):
        B, T = o_lat.shape[:2]
        o = jnp.einsum('bthr,rhv->bthv', o_lat, wuv)
        return o.reshape(B, T, MLA_HEADS * MLA_V) @ wo

    pos_p = jnp.arange(xn_p.shape[1])
    ql_p, qp_p, ckv_p, kpe_p = project(xn_p, pos_p)
    o_p = sweep_query_blocks(lambda qs, pb: mla_attend(qs[0], qs[1], ckv_p, kpe_p, pb, pos_p), (ql_p, qp_p), pos_p)

    t_s = xn_s.shape[1]
    pos_s = past_len + jnp.arange(t_s)
    ql_s, qp_s, ckv_s, kpe_s = project(xn_s, pos_s)
    ckv_all = jnp.concatenate([past_ckv, ckv_s], axis=1)
    kpe_all = jnp.concatenate([past_kpe, kpe_s], axis=1)
    o_s = mla_attend(ql_s, qp_s, ckv_all, kpe_all, pos_s, jnp.arange(past_len + t_s))
    return finish(o_p), finish(o_s), ckv_p, kpe_p, ckv_s, kpe_s


def swa_attend(q, k, v, q_pos, k_pos, sinks, slopes):
    s = jnp.einsum('bnqgrd,bnkgd->bngrqk', q, k).astype(jnp.float32) * (SWA_HD ** -0.5)
    dist = (q_pos[:, :, None] - k_pos[:, None, :]).astype(jnp.float32)
    valid = (dist >= 0) & (dist < WINDOW) & (k_pos[:, None, :] >= 0)
    bias = -slopes.reshape(SWA_KV_HEADS, SWA_REP)[:, :, None, None] * dist[:, None, None]
    s = jnp.where(valid[:, None, None], s + bias, NEG_INF)
    sink = jnp.broadcast_to(sinks.astype(jnp.float32).reshape(1, 1, SWA_KV_HEADS, SWA_REP, 1, 1), s.shape[:-1] + (1,))
    p = jax.nn.softmax(jnp.concatenate([s, sink], axis=-1), axis=-1)[..., :-1]
    return jnp.einsum('bngrqk,bnkgd->bnqgrd', p.astype(v.dtype), v)


def swa_attention(xn_p, xn_s, buf_k, buf_v, past_len, wqkv, bqkv, sinks, wo, bo):
    slopes = alibi_slopes(SWA_HEADS)
    nq = SWA_HEADS * SWA_HD
    nk = SWA_KV_HEADS * SWA_HD

    def project(xn):
        B, T, _ = xn.shape
        qkv = xn @ wqkv + bqkv
        q = qkv[..., :nq].reshape(B, T, SWA_KV_HEADS, SWA_REP, SWA_HD)
        k = qkv[..., nq:nq + nk].reshape(B, T, SWA_KV_HEADS, SWA_HD)
        v = qkv[..., nq + nk:].reshape(B, T, SWA_KV_HEADS, SWA_HD)
        return q, k, v

    def finish(o):
        B = o.shape[0]
        return o.reshape(B, -1, nq) @ wo + bo

    q_p, k_p, v_p = project(xn_p)
    B, T = xn_p.shape[:2]
    nb = T // Q_BLOCK
    qb = q_p.reshape(B, nb, Q_BLOCK, SWA_KV_HEADS, SWA_REP, SWA_HD)

    def band(x):
        prev = jnp.pad(x, ((0, 0), (Q_BLOCK, 0), (0, 0), (0, 0)))[:, :T]
        return jnp.concatenate([prev.reshape(B, nb, Q_BLOCK, SWA_KV_HEADS, SWA_HD),
                                x.reshape(B, nb, Q_BLOCK, SWA_KV_HEADS, SWA_HD)], axis=2)

    blk = jnp.arange(nb)[:, None] * Q_BLOCK
    qpos_b = blk + jnp.arange(Q_BLOCK)[None, :]
    kpos_b = blk - Q_BLOCK + jnp.arange(2 * Q_BLOCK)[None, :]
    o_p = swa_attend(qb, band(k_p), band(v_p), qpos_b, kpos_b, sinks, slopes)

    q_s, k_s, v_s = project(xn_s)
    t_s = xn_s.shape[1]
    k_all = jnp.concatenate([buf_k, k_s], axis=1)
    v_all = jnp.concatenate([buf_v, v_s], axis=1)
    qpos_s = (past_len + jnp.arange(t_s))[None, :]
    kpos_s = (past_len - WINDOW + jnp.arange(WINDOW + t_s))[None, :]
    o_s = swa_attend(q_s[:, None], k_all[:, None], v_all[:, None], qpos_s, kpos_s, sinks, slopes)
    return (finish(o_p), finish(o_s), k_p[:, -WINDOW:], v_p[:, -WINDOW:],
            k_all[:, -WINDOW:], v_all[:, -WINDOW:])


def conv_ffn(xn, prev, w_in, conv_w, conv_b, w_out):
    T = xn.shape[1]
    gu = xn @ w_in
    g, u = gu[..., :D_FF], gu[..., D_FF:]
    gp = jnp.concatenate([prev, g], axis=1)
    gc = conv_b + sum(conv_w[i] * gp[:, i:i + T] for i in range(CONV_W))
    y = (jax.nn.gelu(gc) * u) @ w_out
    return y, gp[:, -(CONV_W - 1):]


def setup_inputs(seed: int = 0) -> dict:
    key = jax.random.key(seed)
    ks = iter(jax.random.split(key, 64))
    f32 = jnp.float32
    n_pages = PAST_LEN // PAGE_SIZE
    n_pool = (DEC_BATCH * n_pages * 5) // 4

    def nrm(shape, scale=1.0):
        return jax.random.normal(next(ks), shape, f32) * scale

    def gain(shape):
        return 1.0 + 0.01 * nrm(shape)

    inp = {}
    inp["x_prompt"] = nrm((BATCH, SEQ, D_MODEL))
    inp["x_sample"] = nrm((DEC_BATCH, DEC_SEQ, D_MODEL))
    inp["cache_diff_k"] = nrm((N_A, n_pool, PAGE_SIZE, DIFF_KV_HEADS, 2, DIFF_HD))
    inp["cache_diff_v"] = nrm((N_A, n_pool, PAGE_SIZE, DIFF_KV_HEADS, 2 * DIFF_HD))
    inp["cache_mla_ckv"] = nrm((N_B, n_pool, PAGE_SIZE, MLA_KV_RANK))
    inp["cache_mla_kpe"] = nrm((N_B, n_pool, PAGE_SIZE, MLA_ROPE))
    inp["state_swa_k"] = nrm((N_C, DEC_BATCH, WINDOW, SWA_KV_HEADS, SWA_HD))
    inp["state_swa_v"] = nrm((N_C, DEC_BATCH, WINDOW, SWA_KV_HEADS, SWA_HD))
    inp["state_ffn_conv"] = nrm((DEPTH, DEC_BATCH, CONV_W - 1, D_FF))
    inp["page_table"] = jax.random.permutation(next(ks), n_pool)[:DEC_BATCH * n_pages].reshape(DEC_BATCH, n_pages).astype(jnp.int32)
    inp["norm_mix"] = gain((DEPTH, D_MODEL))
    inp["norm_ffn"] = gain((DEPTH, D_MODEL))
    inp["norm_final"] = gain((D_MODEL,))
    inp["diff_wq"] = nrm((N_A, D_MODEL, DIFF_HEADS * 2 * DIFF_HD), D_MODEL ** -0.5)
    inp["diff_wk"] = nrm((N_A, D_MODEL, DIFF_KV_HEADS * 2 * DIFF_HD), D_MODEL ** -0.5)
    inp["diff_wv"] = nrm((N_A, D_MODEL, DIFF_KV_HEADS * 2 * DIFF_HD), D_MODEL ** -0.5)
    inp["diff_lq1"] = nrm((N_A, DIFF_HD), 0.1)
    inp["diff_lk1"] = nrm((N_A, DIFF_HD), 0.1)
    inp["diff_lq2"] = nrm((N_A, DIFF_HD), 0.1)
    inp["diff_lk2"] = nrm((N_A, DIFF_HD), 0.1)
    inp["diff_subln"] = gain((N_A, 2 * DIFF_HD))
    inp["diff_wo"] = nrm((N_A, DIFF_HEADS * 2 * DIFF_HD, D_MODEL), (DIFF_HEADS * 2 * DIFF_HD) ** -0.5)
    inp["mla_wdq"] = nrm((N_B, D_MODEL, MLA_Q_RANK), D_MODEL ** -0.5)
    inp["mla_q_norm"] = gain((N_B, MLA_Q_RANK))
    inp["mla_wuq"] = nrm((N_B, MLA_Q_RANK, MLA_HEADS * (MLA_NOPE + MLA_ROPE)), MLA_Q_RANK ** -0.5)
    inp["mla_wdkv"] = nrm((N_B, D_MODEL, MLA_KV_RANK + MLA_ROPE), D_MODEL ** -0.5)
    inp["mla_kv_norm"] = gain((N_B, MLA_KV_RANK))
    inp["mla_wuk"] = nrm((N_B, MLA_KV_RANK, MLA_HEADS, MLA_NOPE), MLA_KV_RANK ** -0.5)
    inp["mla_wuv"] = nrm((N_B, MLA_KV_RANK, MLA_HEADS, MLA_V), MLA_KV_RANK ** -0.5)
    inp["mla_wo"] = nrm((N_B, MLA_HEADS * MLA_V, D_MODEL), (MLA_HEADS * MLA_V) ** -0.5)
    nqkv = (SWA_HEADS + 2 * SWA_KV_HEADS) * SWA_HD
    inp["swa_wqkv"] = nrm((N_C, D_MODEL, nqkv), D_MODEL ** -0.5)
    inp["swa_bqkv"] = nrm((N_C, nqkv), 0.01)
    inp["swa_sinks"] = nrm((N_C, SWA_HEADS), 1.0)
    inp["swa_wo"] = nrm((N_C, SWA_HEADS * SWA_HD, D_MODEL), (SWA_HEADS * SWA_HD) ** -0.5)
    inp["swa_bo"] = nrm((N_C, D_MODEL), 0.01)
    inp["ffn_w_in"] = nrm((DEPTH, D_MODEL, 2 * D_FF), D_MODEL ** -0.5)
    inp["ffn_conv_w"] = nrm((DEPTH, CONV_W, D_FF), CONV_W ** -0.5)
    inp["ffn_conv_b"] = nrm((DEPTH, D_FF), 0.01)
    inp["ffn_w_out"] = nrm((DEPTH, D_FF, D_MODEL), D_FF ** -0.5)
    return inp


def reference(x_prompt, x_sample, cache_diff_k, cache_diff_v, cache_mla_ckv, cache_mla_kpe,
              state_swa_k, state_swa_v, state_ffn_conv, page_table,
              norm_mix, norm_ffn, norm_final,
              diff_wq, diff_wk, diff_wv, diff_lq1, diff_lk1, diff_lq2, diff_lk2, diff_subln, diff_wo,
              mla_wdq, mla_q_norm, mla_wuq, mla_wdkv, mla_kv_norm, mla_wuk, mla_wuv, mla_wo,
              swa_wqkv, swa_bqkv, swa_sinks, swa_wo, swa_bo,
              ffn_w_in, ffn_conv_w, ffn_conv_b, ffn_w_out):
    db = x_sample.shape[0]
    past_len = page_table.shape[1] * PAGE_SIZE
    hp, hs = x_prompt, x_sample
    dkp, dvp, dks, dvs = [], [], [], []
    mcp, mpp, mcs, mps = [], [], [], []
    skp, svp, sks, svs = [], [], [], []
    cvp, cvs = [], []

    for i in range(DEPTH):
        kind, j = i % N_MIXERS, i // N_MIXERS
        np_ = rms_norm(hp, norm_mix[i])
        ns_ = rms_norm(hs, norm_mix[i])
        if kind == 0:
            past_k = cache_diff_k[j, page_table].reshape((db, past_len) + cache_diff_k.shape[3:])
            past_v = cache_diff_v[j, page_table].reshape((db, past_len) + cache_diff_v.shape[3:])
            lam_init = 0.8 - 0.6 * math.exp(-0.3 * i)
            op, os_, a, b, c, d = diff_attention(np_, ns_, past_k, past_v, past_len,
                                                 diff_wq[j], diff_wk[j], diff_wv[j], diff_lq1[j], diff_lk1[j],
                                                 diff_lq2[j], diff_lk2[j], diff_subln[j], diff_wo[j], lam_init)
            dkp.append(a); dvp.append(b); dks.append(c); dvs.append(d)
        elif kind == 1:
            past_c = cache_mla_ckv[j, page_table].reshape(db, past_len, MLA_KV_RANK)
            past_p = cache_mla_kpe[j, page_table].reshape(db, past_len, MLA_ROPE)
            op, os_, a, b, c, d = mla_attention(np_, ns_, past_c, past_p, past_len,
                                                mla_wdq[j], mla_q_norm[j], mla_wuq[j], mla_wdkv[j],
                                                mla_kv_norm[j], mla_wuk[j], mla_wuv[j], mla_wo[j])
            mcp.append(a); mpp.append(b); mcs.append(c); mps.append(d)
        else:
            op, os_, a, b, c, d = swa_attention(np_, ns_, state_swa_k[j], state_swa_v[j], past_len,
                                                swa_wqkv[j], swa_bqkv[j], swa_sinks[j], swa_wo[j], swa_bo[j])
            skp.append(a); svp.append(b); sks.append(c); svs.append(d)
        hp = hp + op
        hs = hs + os_

        zeros_prev = jnp.zeros((hp.shape[0], CONV_W - 1, D_FF), hp.dtype)
        fp, cp = conv_ffn(rms_norm(hp, norm_ffn[i]), zeros_prev, ffn_w_in[i], ffn_conv_w[i], ffn_conv_b[i], ffn_w_out[i])
        fs, cs = conv_ffn(rms_norm(hs, norm_ffn[i]), state_ffn_conv[i], ffn_w_in[i], ffn_conv_w[i], ffn_conv_b[i], ffn_w_out[i])
        hp = hp + fp
        hs = hs + fs
        cvp.append(cp); cvs.append(cs)

    y_prompt = rms_norm(hp, norm_final)
    y_sample = rms_norm(hs, norm_final)
    return (y_prompt, y_sample,
            jnp.stack(dkp), jnp.stack(dvp), jnp.stack(dks), jnp.stack(dvs),
            jnp.stack(mcp), jnp.stack(mpp), jnp.stack(mcs), jnp.stack(mps),
            jnp.stack(skp), jnp.stack(svp), jnp.stack(sks), jnp.stack(svs),
            jnp.stack(cvp), jnp.stack(cvs))
```

```python
import functools
import math

import numpy as np
import jax
import jax.numpy as jnp
from jax import lax
from jax.experimental import pallas as pl
from jax.experimental.pallas import tpu as pltpu

F32 = jnp.float32
BF16 = jnp.bfloat16
NEG = -1e30
LANES = 128
VMEM_LIMIT = 56 * 1024 * 1024

N_MIXERS = 3
DIFF_HEADS, DIFF_G, DIFF_HD = 8, 2, 64
DIFF_R = DIFF_HEADS // DIFF_G
DIFF_SUBLN_EPS = 1e-5
MLA_HEADS, MLA_NOPE, MLA_ROPE, MLA_V = 16, 64, 32, 64
ROPE_THETA = 10000.0
SWA_HEADS, SWA_G, SWA_HD, WINDOW = 16, 2, 64, 128
SWA_R = SWA_HEADS // SWA_G
CONV_W = 3
PAGE = 128
EPS = 1e-6
POS_SPLIT = 64


def _dot(a, b):
    return jnp.dot(a, b, preferred_element_type=F32)


def _dot_nt(a, b):
    return lax.dot_general(a, b, (((1,), (1,)), ((), ())), preferred_element_type=F32)


def _rms_rows(x, g, eps):
    return x * lax.rsqrt(jnp.mean(x * x, axis=-1, keepdims=True) + eps) * g


def _params(sem):
    return pltpu.CompilerParams(dimension_semantics=sem, vmem_limit_bytes=VMEM_LIMIT)


def _const_spec(shape):
    nd = len(shape)
    return pl.BlockSpec(shape, lambda *_: (0,) * nd, pipeline_mode=pl.Buffered(1))


def _proj_body(x_ref, g_ref, wq_ref, qrow_ref, wk_ref, krow_ref, kaug_ref, wkv_ref, kvrow_ref,
               qp_ref, kp_ref, kT_ref, v_ref, vT32_ref, vTb_ref, *, tk):
    tm = x_ref.shape[0]
    c = kT_ref.shape[1]
    xn = _rms_rows(x_ref[...], g_ref[...], EPS).astype(BF16)
    q = _dot(xn, wq_ref[...]) + qrow_ref[...]
    for h in range(qp_ref.shape[0]):
        qp_ref[h] = q[:, h * LANES:(h + 1) * LANES].astype(BF16)
    k2 = _dot(xn, wk_ref[...]) + krow_ref[...]
    kaug = kaug_ref[...]
    for j in range(kp_ref.shape[0]):
        kp_ref[j] = (k2[:, j * LANES:(j + 1) * LANES] + kaug).astype(BF16)
    kv = _dot(xn, wkv_ref[...]) + kvrow_ref[...]
    v = kv[:, c:]
    v_ref[...] = v
    kT_ref[0] = kv[:, :c].T
    vT = v.T
    vT32_ref[0] = vT
    vTb = vT.astype(BF16)
    for s in range(tm // tk):
        vTb_ref[s] = vTb[:, s * tk:(s + 1) * tk]


def _proj(x, g, wq, qrow, wk, krow, kaug, wkv, kvrow, *, batch, tk):
    m, d = x.shape
    t = m // batch
    tm = min(512, t)
    nq, nk, c = wq.shape[1] // LANES, wk.shape[1] // LANES, wkv.shape[1] // 2
    tps = t // tm
    grid = (m // tm,)
    row = lambda i: (i, 0)
    return pl.pallas_call(
        functools.partial(_proj_body, tk=tk),
        grid=grid,
        in_specs=[pl.BlockSpec((tm, d), row), _const_spec((1, d)),
                  _const_spec(wq.shape), _const_spec(qrow.shape),
                  _const_spec(wk.shape), _const_spec(krow.shape),
                  pl.BlockSpec((tm, LANES), row),
                  _const_spec(wkv.shape), _const_spec(kvrow.shape)],
        out_specs=[pl.BlockSpec((nq, tm, LANES), lambda i: (0, i, 0)),
                   pl.BlockSpec((nk, tm, LANES), lambda i: (0, i, 0)),
                   pl.BlockSpec((1, c, tm), lambda i: (i // tps, 0, i % tps)),
                   pl.BlockSpec((tm, c), row),
                   pl.BlockSpec((1, c, tm), lambda i: (i // tps, 0, i % tps)),
                   pl.BlockSpec((tm // tk, c, tk), lambda i: (i, 0, 0))],
        out_shape=[jax.ShapeDtypeStruct((nq, m, LANES), BF16),
                   jax.ShapeDtypeStruct((nk, m, LANES), BF16),
                   jax.ShapeDtypeStruct((batch, c, t), F32),
                   jax.ShapeDtypeStruct((m, c), F32),
                   jax.ShapeDtypeStruct((batch, c, t), F32),
                   jax.ShapeDtypeStruct((m // tk, c, tk), BF16)],
        compiler_params=_params(("arbitrary",)),
        name="proj",
    )(x, g, wq, qrow, wk, krow, kaug, wkv, kvrow)


def _flash_group(q, k_ref, vT_ref, i, tq, acc_ref, m_ref, l_ref):
    kd = k_ref[pl.ds(pl.multiple_of(i * tq, tq), tq), :]
    s = _dot_nt(kd, q)
    krow = lax.broadcasted_iota(jnp.int32, s.shape, 0)
    qcol = lax.broadcasted_iota(jnp.int32, s.shape, 1) & (tq - 1)
    s = jnp.where(krow <= qcol, s, NEG)
    m0 = jnp.max(s, axis=0, keepdims=True)
    p = jnp.exp(s - m0)
    m_ref[...] = m0
    l_ref[...] = jnp.sum(p, axis=0, keepdims=True)
    acc_ref[...] = _dot(vT_ref[i], p.astype(BF16))

    def step(j, carry):
        kj = k_ref[pl.ds(pl.multiple_of(j * tq, tq), tq), :]
        sj = _dot_nt(kj, q)
        m_old = m_ref[...]
        m_new = jnp.maximum(m_old, jnp.max(sj, axis=0, keepdims=True))
        alpha = jnp.exp(m_old - m_new)
        pj = jnp.exp(sj - m_new)
        l_ref[...] = alpha * l_ref[...] + jnp.sum(pj, axis=0, keepdims=True)
        acc_ref[...] = alpha * acc_ref[...] + _dot(vT_ref[j], pj.astype(BF16))
        m_ref[...] = m_new
        return carry

    lax.fori_loop(0, i, step, 0)
    return acc_ref[...] / l_ref[...]


def _diff_lambda(lam4, lam_init):
    a = jnp.sum(lam4[0:1] * lam4[1:2], axis=-1, keepdims=True)
    b = jnp.sum(lam4[2:3] * lam4[3:4], axis=-1, keepdims=True)
    return jnp.exp(a) - jnp.exp(b) + lam_init


def _flash_diff_body(qp_ref, kp_ref, vT_ref, lam4_ref, subln_ref, out_ref, acc_ref, m_ref, l_ref,
                     *, tq, lam_init):
    i = pl.program_id(2)
    r = DIFF_R
    o = []
    for mm in range(2):
        q = qp_ref[mm * r:(mm + 1) * r].reshape(r * tq, LANES)
        o.append(_flash_group(q, kp_ref.at[mm], vT_ref, i, tq, acc_ref, m_ref, l_ref))
    lam = _diff_lambda(lam4_ref[...], lam_init)
    od = o[0] - lam * o[1]
    ms = jnp.mean(od * od, axis=0, keepdims=True)
    od = od * lax.rsqrt(ms + DIFF_SUBLN_EPS) * subln_ref[...] * (1.0 - lam_init)
    for rr in range(r):
        out_ref[:, rr * LANES:(rr + 1) * LANES] = od[:, rr * tq:(rr + 1) * tq].T.astype(out_ref.dtype)


def _flash_diff(qp, kp, vTb, lam4, subln_col, *, batch, tq, lam_init):
    nh, m, _ = qp.shape
    t = m // batch
    nq = t // tq
    r = DIFF_R
    return pl.pallas_call(
        functools.partial(_flash_diff_body, tq=tq, lam_init=lam_init),
        grid=(batch, DIFF_G, nq),
        in_specs=[pl.BlockSpec((2 * r, tq, LANES), lambda b, g, i: (g, b * nq + i, 0)),
                  pl.BlockSpec((2, t, LANES), lambda b, g, i: (g, b, 0)),
                  pl.BlockSpec((nq, 2 * DIFF_HD, tq), lambda b, g, i: (b, g, 0)),
                  pl.BlockSpec(lam4.shape, lambda b, g, i: (0, 0)),
                  pl.BlockSpec(subln_col.shape, lambda b, g, i: (0, 0))],
        out_specs=pl.BlockSpec((tq, r * LANES), lambda b, g, i: (b * nq + i, g)),
        out_shape=jax.ShapeDtypeStruct((m, DIFF_G * r * LANES), BF16),
        scratch_shapes=[pltpu.VMEM((2 * DIFF_HD, r * tq), F32),
                        pltpu.VMEM((1, r * tq), F32), pltpu.VMEM((1, r * tq), F32)],
        compiler_params=_params(("arbitrary", "arbitrary", "arbitrary")),
        name="flash_diff",
    )(qp, kp, vTb, lam4, subln_col)


MLA_GRP = 4


def _flash_mla_body(qp_ref, kp_ref, vT_ref, wuvT_ref, out_ref, acc_ref, m_ref, l_ref, *, tq):
    i = pl.program_id(2)
    dk = qp_ref.shape[-1]
    q = qp_ref[...].reshape(MLA_GRP * tq, dk)
    oT = _flash_group(q, kp_ref.at[0], vT_ref, i, tq, acc_ref, m_ref, l_ref).astype(BF16)
    for pr in range(MLA_GRP // 2):
        parts = [_dot(wuvT_ref[2 * pr + e], oT[:, (2 * pr + e) * tq:(2 * pr + e + 1) * tq]) for e in range(2)]
        blk = jnp.concatenate(parts, axis=0)
        out_ref[:, pr * LANES:(pr + 1) * LANES] = blk.T.astype(out_ref.dtype)


def _flash_mla(qp, kp, ckvTb, wuvT, *, batch, tq):
    nh, m, dk = qp.shape
    t = m // batch
    nq = t // tq
    r = ckvTb.shape[1]
    return pl.pallas_call(
        functools.partial(_flash_mla_body, tq=tq),
        grid=(batch, nh // MLA_GRP, nq),
        in_specs=[pl.BlockSpec((MLA_GRP, tq, dk), lambda b, g, i: (g, b * nq + i, 0)),
                  pl.BlockSpec((1, t, dk), lambda b, g, i: (0, b, 0)),
                  pl.BlockSpec((nq, r, tq), lambda b, g, i: (b, 0, 0)),
                  pl.BlockSpec((MLA_GRP, MLA_V, r), lambda b, g, i: (g, 0, 0))],
        out_specs=pl.BlockSpec((tq, MLA_GRP * MLA_V), lambda b, g, i: (b * nq + i, g)),
        out_shape=jax.ShapeDtypeStruct((m, nh * MLA_V), BF16),
        scratch_shapes=[pltpu.VMEM((r, MLA_GRP * tq), F32),
                        pltpu.VMEM((1, MLA_GRP * tq), F32), pltpu.VMEM((1, MLA_GRP * tq), F32)],
        compiler_params=_params(("arbitrary", "arbitrary", "arbitrary")),
        name="flash_mla",
    )(qp, kp, ckvTb, wuvT)


def _proj_mla_body(x_ref, g_ref, wdq_ref, qn_ref, wn_ref, wr_ref, wrs_ref, wukbd_ref,
                   wc_ref, kvn_ref, wkr_ref, wkrs_ref, cos_ref, sin_ref,
                   qp_ref, kp_ref, ckv_ref, kpeT_ref, ckvTb_ref, *, tk):
    tm = x_ref.shape[0]
    cos, sin = cos_ref[...], sin_ref[...]
    xn = _rms_rows(x_ref[...], g_ref[...], EPS).astype(BF16)
    cq = _rms_rows(_dot(xn, wdq_ref[...]), qn_ref[...], EPS).astype(BF16)
    qn = _dot(cq, wn_ref[...]).astype(BF16)
    qr = _dot(cq, wr_ref[...])
    qrs = _dot(cq, wrs_ref[...])
    for pr in range(wukbd_ref.shape[0]):
        ql = _dot(qn[:, pr * LANES:(pr + 1) * LANES], wukbd_ref[pr])
        for e in range(2):
            h = 2 * pr + e
            hs = slice(h * LANES, (h + 1) * LANES)
            qp_ref[h, :, 0:LANES] = ql[:, e * LANES:(e + 1) * LANES].astype(BF16)
            qp_ref[h, :, LANES:2 * LANES] = (qr[:, hs] * cos + qrs[:, hs] * sin).astype(BF16)
    ckv = _rms_rows(_dot(xn, wc_ref[...]), kvn_ref[...], EPS)
    kr = _dot(xn, wkr_ref[...]) * cos + _dot(xn, wkrs_ref[...]) * sin
    ckv_ref[...] = ckv
    kp_ref[0, :, 0:LANES] = ckv.astype(BF16)
    kp_ref[0, :, LANES:2 * LANES] = kr.astype(BF16)
    kpeT_ref[0] = kr.T[0:MLA_ROPE, :]
    ckvT = ckv.T.astype(BF16)
    for s in range(tm // tk):
        ckvTb_ref[s] = ckvT[:, s * tk:(s + 1) * tk]


def _proj_mla(x, g, w, cos, sin, *, batch, tk):
    m, d = x.shape
    t = m // batch
    tm = min(512, t)
    tps = t // tm
    row = lambda i: (i, 0)
    r = w["wc"].shape[1]
    consts = [w["wdq"], w["qn"], w["wn"], w["wr"], w["wrs"], w["wukbd"], w["wc"], w["kvn"], w["wkr"], w["wkrs"]]
    return pl.pallas_call(
        functools.partial(_proj_mla_body, tk=tk),
        grid=(m // tm,),
        in_specs=[pl.BlockSpec((tm, d), row), _const_spec((1, d))] + [_const_spec(a.shape) for a in consts]
                 + [pl.BlockSpec((tm, LANES), row), pl.BlockSpec((tm, LANES), row)],
        out_specs=[pl.BlockSpec((MLA_HEADS, tm, 2 * LANES), lambda i: (0, i, 0)),
                   pl.BlockSpec((1, tm, 2 * LANES), lambda i: (0, i, 0)),
                   pl.BlockSpec((tm, r), row),
                   pl.BlockSpec((1, MLA_ROPE, tm), lambda i: (i // tps, 0, i % tps)),
                   pl.BlockSpec((tm // tk, r, tk), lambda i: (i, 0, 0))],
        out_shape=[jax.ShapeDtypeStruct((MLA_HEADS, m, 2 * LANES), BF16),
                   jax.ShapeDtypeStruct((1, m, 2 * LANES), BF16),
                   jax.ShapeDtypeStruct((m, r), F32),
                   jax.ShapeDtypeStruct((batch, MLA_ROPE, t), F32),
                   jax.ShapeDtypeStruct((m // tk, r, tk), BF16)],
        compiler_params=_params(("arbitrary",)),
        name="proj_mla",
    )(x, g, *consts, cos, sin)


def _uv_body(o_ref, wuv_ref, out_ref):
    for h in range(o_ref.shape[0]):
        out_ref[h] = _dot(o_ref[h].astype(BF16), wuv_ref[h]).astype(out_ref.dtype)


def _uv_sample(o_lat, wuv):
    h, m, _ = o_lat.shape
    return pl.pallas_call(
        _uv_body,
        out_shape=jax.ShapeDtypeStruct((h, m, wuv.shape[2]), BF16),
        compiler_params=pltpu.CompilerParams(vmem_limit_bytes=VMEM_LIMIT),
        name="uv_sample",
    )(o_lat, wuv)


def _ffn_body(h_ref, a_ref, wo_ref, bo_ref, g_ref, win_ref, cw_ref, cb_ref, wout_ref, gfin_ref, prev_ref,
              out_ref, cst_ref, gs_ref, carry_ref, *, time_major, tiles_per_seq, final, fc):
    i = pl.program_id(0)
    tm = h_ref.shape[0]
    dff = wout_ref.shape[0]
    h1 = h_ref[...] + _dot(a_ref[...], wo_ref[...]) + bo_ref[...]
    xn = _rms_rows(h1, g_ref[...], EPS).astype(BF16)
    if not time_major:
        @pl.when(i % tiles_per_seq == 0)
        def _():
            carry_ref[...] = jnp.zeros(carry_ref.shape, F32)
    acc = h1
    for c in range(dff // fc):
        cs = slice(c * fc, (c + 1) * fc)
        gch = _dot(xn, win_ref[:, cs])
        uch = _dot(xn, win_ref[:, dff + c * fc:dff + (c + 1) * fc])
        cw = cw_ref[:, cs]
        if time_major:
            nb = prev_ref.shape[0] // (CONV_W - 1)
            gs_ref[0:2 * nb, :] = prev_ref[:, cs]
            gs_ref[2 * nb:2 * nb + tm, :] = gch
            gc = cb_ref[:, cs] + cw[0:1] * gs_ref[0:tm, :] + cw[1:2] * gs_ref[nb:nb + tm, :] + cw[2:3] * gch
            cst_ref[:, cs] = gch[tm - 2 * nb:tm, :]
        else:
            gs_ref[0:8, :] = carry_ref[:, cs]
            gs_ref[8:8 + tm, :] = gch
            gc = cb_ref[:, cs] + cw[0:1] * gs_ref[6:6 + tm, :] + cw[1:2] * gs_ref[7:7 + tm, :] + cw[2:3] * gch
            carry_ref[:, cs] = gch[tm - 8:tm, :]
        hid = (jax.nn.gelu(gc) * uch).astype(BF16)
        acc = acc + _dot(hid, wout_ref[cs, :])
    if not time_major:
        cst_ref[0] = carry_ref[...]
    if final:
        acc = _rms_rows(acc, gfin_ref[...], EPS)
    out_ref[...] = acc


def _ffn(h, a, wo, bo, g, win, cw, cb, wout, gfin, prev, *, time_major, seq, final):
    m, d = h.shape
    dff = wout.shape[0]
    fc = 256
    if time_major:
        tm, tiles_per_seq = m, 1
        cst_shape = (prev.shape[0], dff)
        cst_spec = pl.BlockSpec(cst_shape, lambda i: (0, 0))
        gs_rows = prev.shape[0] + tm
    else:
        tm = min(512, seq)
        tiles_per_seq = seq // tm
        cst_shape = (m // tm, 8, dff)
        cst_spec = pl.BlockSpec((1, 8, dff), lambda i: (i, 0, 0))
        gs_rows = 8 + tm
    row = lambda i: (i, 0)
    return pl.pallas_call(
        functools.partial(_ffn_body, time_major=time_major, tiles_per_seq=tiles_per_seq, final=final, fc=fc),
        grid=(m // tm,),
        in_specs=[pl.BlockSpec((tm, d), row), pl.BlockSpec((tm, a.shape[1]), row),
                  _const_spec(wo.shape), _const_spec(bo.shape), _const_spec(g.shape),
                  _const_spec(win.shape), _const_spec(cw.shape), _const_spec(cb.shape),
                  _const_spec(wout.shape), _const_spec(gfin.shape), _const_spec(prev.shape)],
        out_specs=[pl.BlockSpec((tm, d), row), cst_spec],
        out_shape=[jax.ShapeDtypeStruct((m, d), F32), jax.ShapeDtypeStruct(cst_shape, F32)],
        scratch_shapes=[pltpu.VMEM((gs_rows, fc), F32), pltpu.VMEM((8, dff), F32)],
        compiler_params=_params(("arbitrary",)),
        name="ffn_sample" if time_major else "ffn",
    )(h, a, wo, bo, g, win, cw, cb, wout, gfin, prev)


def _swa_prompt_body(qp_ref, kprev_ref, kcur_ref, vprev_ref, vcur_ref, bias_ref, sink_ref, out_ref):
    i = pl.program_id(1)
    w = kcur_ref.shape[1]
    r = SWA_R
    for g in range(SWA_G):
        q = qp_ref[g * r:(g + 1) * r].reshape(r * w, LANES)
        kcat = jnp.concatenate([kprev_ref[g], kcur_ref[g]], axis=0)
        s = _dot_nt(kcat, q) + bias_ref[g]
        krow = lax.broadcasted_iota(jnp.int32, s.shape, 0)
        s = jnp.where(jnp.logical_and(i == 0, krow < w), NEG, s)
        sink = sink_ref[g]
        mx = jnp.maximum(jnp.max(s, axis=0, keepdims=True), sink)
        p = jnp.exp(s - mx)
        l = jnp.sum(p, axis=0, keepdims=True) + jnp.exp(sink - mx)
        hs = slice(g * SWA_HD, (g + 1) * SWA_HD)
        vcat = jnp.concatenate([vprev_ref[hs, :], vcur_ref[hs, :]], axis=1)
        oT = _dot(vcat, p.astype(BF16)) / l
        for pr in range(r // 2):
            blk = jnp.concatenate([oT[:, (2 * pr) * w:(2 * pr + 1) * w], oT[:, (2 * pr + 1) * w:(2 * pr + 2) * w]], axis=0)
            c0 = (g * r + 2 * pr) * SWA_HD
            out_ref[:, c0:c0 + 2 * SWA_HD] = blk.T.astype(out_ref.dtype)


def _swa_prompt(qp, kp, vTb, bias, sink, *, batch):
    nh, m, _ = qp.shape
    w = WINDOW
    t = m // batch
    nq = t // w
    c = SWA_G * SWA_HD
    vT2 = jnp.transpose(vTb, (1, 0, 2)).reshape(c, m)
    cur = lambda b, i: (0, b * nq + i, 0)
    prv = lambda b, i: (0, b * nq + jnp.maximum(i - 1, 0), 0)
    return pl.pallas_call(
        _swa_prompt_body,
        grid=(batch, nq),
        in_specs=[pl.BlockSpec((nh, w, LANES), cur),
                  pl.BlockSpec((SWA_G, w, LANES), prv), pl.BlockSpec((SWA_G, w, LANES), cur),
                  pl.BlockSpec((c, w), lambda b, i: (0, b * nq + jnp.maximum(i - 1, 0))),
                  pl.BlockSpec((c, w), lambda b, i: (0, b * nq + i)),
                  pl.BlockSpec(bias.shape, lambda b, i: (0, 0, 0)),
                  pl.BlockSpec(sink.shape, lambda b, i: (0, 0, 0))],
        out_specs=pl.BlockSpec((w, nh * SWA_HD), lambda b, i: (b * nq + i, 0)),
        out_shape=jax.ShapeDtypeStruct((m, nh * SWA_HD), BF16),
        compiler_params=_params(("arbitrary", "arbitrary")),
        name="swa_prompt",
    )(qp, kp, kp, vT2, vT2, bias, sink)


def _swa_decode_body(q_ref, kT_ref, vT_ref, kn_ref, vn_ref, bold_ref, bnew_ref, sink_ref, o_ref):
    for b in range(q_ref.shape[0]):
        q = q_ref[b]
        s_old = _dot(q, kT_ref[b].astype(BF16)) + bold_ref[...]
        s_new = _dot_nt(q, kn_ref[b].astype(BF16)) + bnew_ref[...]
        sink = sink_ref[...]
        mx = jnp.maximum(jnp.maximum(jnp.max(s_old, axis=1, keepdims=True), jnp.max(s_new, axis=1, keepdims=True)), sink)
        p_old = jnp.exp(s_old - mx)
        p_new = jnp.exp(s_new - mx)
        l = jnp.sum(p_old, axis=1, keepdims=True) + jnp.sum(p_new, axis=1, keepdims=True) + jnp.exp(sink - mx)
        o = _dot_nt(p_old.astype(BF16), vT_ref[b].astype(BF16)) + _dot(p_new.astype(BF16), vn_ref[b].astype(BF16))
        o_ref[b] = o / l


def _swa_decode(qbd, kT, vT, kn, vn, bold, bnew, sink):
    nb = qbd.shape[0]
    bb = min(8, nb)
    rows = qbd.shape[1]
    blk = lambda shp: pl.BlockSpec((bb,) + shp, lambda i: (i, 0, 0))
    cst = lambda a: pl.BlockSpec(a.shape, lambda i: (0, 0))
    return pl.pallas_call(
        _swa_decode_body,
        grid=(nb // bb,),
        in_specs=[blk(qbd.shape[1:]), blk(kT.shape[1:]), blk(vT.shape[1:]), blk(kn.shape[1:]), blk(vn.shape[1:]),
                  cst(bold), cst(bnew), cst(sink)],
        out_specs=blk((rows, kT.shape[1])),
        out_shape=jax.ShapeDtypeStruct((nb, rows, kT.shape[1]), F32),
        compiler_params=_params(("arbitrary",)),
        name="swa_decode",
    )(qbd, kT, vT, kn, vn, bold, bnew, sink)


def _page_copies(pt_ref, srcs, bufs, sem, layer, gidx, slot, nc, ch):
    b = gidx // nc
    c = gidx % nc
    out = []
    for p in range(ch):
        pg = pt_ref[b, c * ch + p]
        for src, buf in zip(srcs, bufs):
            out.append(pltpu.make_async_copy(src.at[layer, pg], buf.at[slot, p], sem.at[slot]))
    return out


def _decode_diff_body(pt_ref, kc_ref, vc_ref, q_ref, kn_ref, vn_ref, slope_ref, lam4_ref, subln_ref, o_ref,
                      kbuf, vbuf, sem, *, layer, ch, nc, nb, past_len, lam_init):
    total = nb * nc
    rows = q_ref.shape[1]
    hg = rows // DIFF_G
    slope = slope_ref[...]
    copies = functools.partial(_page_copies, pt_ref, (kc_ref, vc_ref), (kbuf, vbuf), sem, layer, nc=nc, ch=ch)

    for cp in copies(0, 0):
        cp.start()

    def chunk(gidx, slot, c, q, carry):
        m, l, acc = carry

        @pl.when(gidx + 1 < total)
        def _():
            for cp in copies(gidx + 1, 1 - slot):
                cp.start()

        for cp in copies(gidx, slot):
            cp.wait()
        s = jnp.concatenate([_dot(q, kbuf[slot, p].astype(BF16)) for p in range(ch)], axis=1)
        kpos = c * (ch * PAGE) + lax.broadcasted_iota(jnp.int32, (1, ch * PAGE), 1)
        s = s + slope * kpos.astype(F32)
        m_new = jnp.maximum(m, jnp.max(s, axis=1, keepdims=True))
        alpha = jnp.exp(m - m_new)
        p = jnp.exp(s - m_new)
        l = alpha * l + jnp.sum(p, axis=1, keepdims=True)
        pb = p.astype(BF16)
        pv = []
        for g in range(DIFF_G):
            vg = vbuf[slot, :, pl.ds(g, PAGE, stride=DIFF_G), :].reshape(ch * PAGE, LANES).astype(BF16)
            pv.append(_dot(pb[g * hg:(g + 1) * hg], vg))
        acc = alpha * acc + jnp.concatenate(pv, axis=0)
        return m_new, l, acc

    def batch_step(b, carry):
        q = q_ref[b]

        def pair(cc, st):
            for half in range(2):
                c = 2 * cc + half
                st = chunk(b * nc + c, half, c, q, st)
            return st

        init = (jnp.full((rows, 1), NEG, F32), jnp.zeros((rows, 1), F32), jnp.zeros((rows, LANES), F32))
        m, l, acc = lax.fori_loop(0, nc // 2, pair, init)
        s_n = _dot_nt(q, kn_ref[b].astype(BF16))
        tn = lax.broadcasted_iota(jnp.int32, s_n.shape, 1)
        tq = lax.broadcasted_iota(jnp.int32, s_n.shape, 0) & 3
        s_n = jnp.where(tn <= tq, s_n + slope * (past_len + tn).astype(F32), NEG)
        m_new = jnp.maximum(m, jnp.max(s_n, axis=1, keepdims=True))
        alpha = jnp.exp(m - m_new)
        p_n = jnp.exp(s_n - m_new)
        l = alpha * l + jnp.sum(p_n, axis=1, keepdims=True)
        pnb = p_n.astype(BF16)
        pv = [_dot(pnb[g * hg:(g + 1) * hg], vn_ref[b, g].astype(BF16)) for g in range(DIFF_G)]
        o = (alpha * acc + jnp.concatenate(pv, axis=0)) / l
        lam = _diff_lambda(lam4_ref[...], lam_init)
        hh = hg // 2
        for g in range(DIFF_G):
            d = o[g * hg:g * hg + hh] - lam * o[g * hg + hh:(g + 1) * hg]
            d = _rms_rows(d, subln_ref[...], DIFF_SUBLN_EPS) * (1.0 - lam_init)
            o_ref[b, g] = d
        return carry

    lax.fori_loop(0, nb, batch_step, 0)


def _decode_diff(pt, kc, vc, qbd, kn, vn, slope, lam4, subln, *, layer, lam_init):
    nb, npages = pt.shape
    ch = min(8, npages // 2)
    nc = npages // ch
    rows = qbd.shape[1]
    vm = lambda a: pl.BlockSpec(a.shape, lambda i, pt_: (0,) * a.ndim)
    return pl.pallas_call(
        functools.partial(_decode_diff_body, layer=layer, ch=ch, nc=nc, nb=nb, past_len=npages * PAGE, lam_init=lam_init),
        grid_spec=pltpu.PrefetchScalarGridSpec(
            num_scalar_prefetch=1, grid=(1,),
            in_specs=[pl.BlockSpec(memory_space=pl.ANY), pl.BlockSpec(memory_space=pl.ANY),
                      vm(qbd), vm(kn), vm(vn), vm(slope), vm(lam4), vm(subln)],
            out_specs=pl.BlockSpec((nb, DIFF_G, rows // 4, LANES), lambda i, pt_: (0, 0, 0, 0)),
            scratch_shapes=[pltpu.VMEM((2, ch) + kc.shape[2:], F32), pltpu.VMEM((2, ch) + vc.shape[2:], F32),
                            pltpu.SemaphoreType.DMA((2,))]),
        out_shape=jax.ShapeDtypeStruct((nb, DIFF_G, rows // 4, LANES), F32),
        compiler_params=_params(("arbitrary",)),
        name="decode_diff",
    )(pt, kc, vc, qbd, kn, vn, slope, lam4, subln)


def _decode_mla_body(pt_ref, cc_ref, pc_ref, ql_ref, qp_ref, cn_ref, pn_ref, o_ref, cbuf, pbuf, sem,
                     *, layer, ch, nc, nb, t_s):
    total = nb * nc
    rows = ql_ref.shape[1]
    r = cbuf.shape[-1]
    copies = functools.partial(_page_copies, pt_ref, (cc_ref, pc_ref), (cbuf, pbuf), sem, layer, nc=nc, ch=ch)

    for cp in copies(0, 0):
        cp.start()

    def chunk(gidx, slot, ql, qpe, carry):
        m, l, acc = carry

        @pl.when(gidx + 1 < total)
        def _():
            for cp in copies(gidx + 1, 1 - slot):
                cp.start()

        for cp in copies(gidx, slot):
            cp.wait()
        ckv = cbuf[slot].reshape(ch * PAGE, r).astype(BF16)
        s = _dot_nt(ql, ckv) + jnp.concatenate([_dot(qpe, pbuf[slot, p].astype(BF16)) for p in range(ch)], axis=1)
        m_new = jnp.maximum(m, jnp.max(s, axis=1, keepdims=True))
        alpha = jnp.exp(m - m_new)
        p = jnp.exp(s - m_new)
        l = alpha * l + jnp.sum(p, axis=1, keepdims=True)
        acc = alpha * acc + _dot(p.astype(BF16), ckv)
        return m_new, l, acc

    def batch_step(b, carry):
        ql = ql_ref[b]
        qpe = qp_ref[b]

        def pair(cc, st):
            for half in range(2):
                st = chunk(b * nc + 2 * cc + half, half, ql, qpe, st)
            return st

        init = (jnp.full((rows, 1), NEG, F32), jnp.zeros((rows, 1), F32), jnp.zeros((rows, r), F32))
        m, l, acc = lax.fori_loop(0, nc // 2, pair, init)
        cn = cn_ref[b].astype(BF16)
        s_n = _dot_nt(ql, cn) + _dot_nt(qpe, pn_ref[b].astype(BF16))
        tn = lax.broadcasted_iota(jnp.int32, s_n.shape, 1)
        tq = lax.broadcasted_iota(jnp.int32, s_n.shape, 0) // (rows // t_s)
        s_n = jnp.where(tn <= tq, s_n, NEG)
        m_new = jnp.maximum(m, jnp.max(s_n, axis=1, keepdims=True))
        alpha = jnp.exp(m - m_new)
        p_n = jnp.exp(s_n - m_new)
        l = alpha * l + jnp.sum(p_n, axis=1, keepdims=True)
        o_ref[b] = (alpha * acc + _dot(p_n.astype(BF16), cn)) / l
        return carry

    lax.fori_loop(0, nb, batch_step, 0)


def _decode_mla(pt, cc, pc, ql, qpe, cn, pn, *, layer, t_s):
    nb, npages = pt.shape
    ch = min(16, npages // 2)
    nc = npages // ch
    rows = ql.shape[1]
    vm = lambda a: pl.BlockSpec(a.shape, lambda i, pt_: (0,) * a.ndim)
    return pl.pallas_call(
        functools.partial(_decode_mla_body, layer=layer, ch=ch, nc=nc, nb=nb, t_s=t_s),
        grid_spec=pltpu.PrefetchScalarGridSpec(
            num_scalar_prefetch=1, grid=(1,),
            in_specs=[pl.BlockSpec(memory_space=pl.ANY), pl.BlockSpec(memory_space=pl.ANY),
                      vm(ql), vm(qpe), vm(cn), vm(pn)],
            out_specs=pl.BlockSpec((nb, rows, cc.shape[-1]), lambda i, pt_: (0, 0, 0)),
            scratch_shapes=[pltpu.VMEM((2, ch) + cc.shape[2:], F32), pltpu.VMEM((2, ch) + pc.shape[2:], F32),
                            pltpu.SemaphoreType.DMA((2,))]),
        out_shape=jax.ShapeDtypeStruct((nb, rows, cc.shape[-1]), F32),
        compiler_params=_params(("arbitrary",)),
        name="decode_mla",
    )(pt, cc, pc, ql, qpe, cn, pn)


def _alibi(n):
    return np.array([2.0 ** (-8.0 * (h + 1) / n) for h in range(n)], dtype=np.float32)


def _pad_heads(w, nh, hd, scale=1.0):
    d = w.shape[0]
    w3 = (w * scale).reshape(d, nh, hd)
    return jnp.pad(w3, ((0, 0), (0, 0), (0, LANES - hd))).reshape(d, nh * LANES).astype(BF16)


def _pos_aug(pos):
    hi = (pos // POS_SPLIT).astype(F32)
    lo = (pos % POS_SPLIT).astype(F32)
    cols = jnp.stack([hi, hi, lo, lo], axis=1)
    return jnp.pad(cols, ((0, 0), (DIFF_HD, LANES - DIFF_HD - 4)))


def _slope_aug(slopes):
    s = jnp.asarray(slopes, F32)
    hi = s.astype(BF16).astype(F32)
    lo = (s - hi).astype(BF16).astype(F32)
    cols = jnp.stack([hi * POS_SPLIT, lo * POS_SPLIT, hi, lo], axis=1)
    return jnp.pad(cols, ((0, 0), (DIFF_HD, LANES - DIFF_HD - 4))).reshape(1, -1)


def _rope_tables(pos):
    half = MLA_ROPE // 2
    freqs = ROPE_THETA ** (-jnp.arange(half, dtype=F32) * 2.0 / MLA_ROPE)
    ang = pos.astype(F32)[:, None] * freqs[None, :]
    cos = jnp.repeat(jnp.cos(ang), 2, axis=1)
    sin = jnp.stack([-jnp.sin(ang), jnp.sin(ang)], axis=-1).reshape(-1, MLA_ROPE)
    pad = ((0, 0), (0, LANES - MLA_ROPE))
    return jnp.pad(cos, pad), jnp.pad(sin, pad)


def _swap_pairs(w):
    return w.reshape(w.shape[:-1] + (w.shape[-1] // 2, 2))[..., ::-1].reshape(w.shape)


def _tm(x):
    return jnp.swapaxes(x, 0, 1).reshape((x.shape[0] * x.shape[1],) + x.shape[2:])


def kernel(x_prompt, x_sample, cache_diff_k, cache_diff_v, cache_mla_ckv, cache_mla_kpe, state_swa_k, state_swa_v,
           state_ffn_conv, page_table, norm_mix, norm_ffn, norm_final, diff_wq, diff_wk, diff_wv, diff_lq1, diff_lk1,
           diff_lq2, diff_lk2, diff_subln, diff_wo, mla_wdq, mla_q_norm, mla_wuq, mla_wdkv, mla_kv_norm, mla_wuk,
           mla_wuv, mla_wo, swa_wqkv, swa_bqkv, swa_sinks, swa_wo, swa_bo, ffn_w_in, ffn_conv_w, ffn_conv_b, ffn_w_out):
    bsz, seq, d = x_prompt.shape
    nb, t_s, _ = x_sample.shape
    depth = norm_mix.shape[0]
    dff = ffn_w_out.shape[1]
    npages = page_table.shape[1]
    past_len = npages * PAGE
    n_pool = cache_diff_k.shape[1]
    mp, ms = bsz * seq, nb * t_s
    tq = min(256, seq)

    hp = x_prompt.reshape(mp, d)
    hs = _tm(x_sample)
    pos_p = jnp.tile(jnp.arange(seq, dtype=jnp.int32), bsz)
    pos_s = past_len + jnp.repeat(jnp.arange(t_s, dtype=jnp.int32), nb)
    zrow = lambda n: jnp.zeros((1, n), F32)

    kc = jnp.transpose(cache_diff_k, (0, 1, 3, 4, 5, 2)).reshape(cache_diff_k.shape[0], n_pool, 2 * DIFF_G * DIFF_HD, PAGE)
    vc = cache_diff_v.reshape(cache_diff_v.shape[0], n_pool, PAGE * DIFF_G, 2 * DIFF_HD)
    pc = jnp.transpose(cache_mla_kpe, (0, 1, 3, 2))

    outs = {k: [] for k in ("dkp", "dvp", "dks", "dvs", "mcp", "mpp", "mcs", "mps", "skp", "svp", "sks", "svs", "cvp", "cvs")}
    a_p = a_s = None

    for i in range(depth):
        kind, j = i % N_MIXERS, i // N_MIXERS
        gmix = norm_mix[i].reshape(1, d)
        if kind == 0:
            lam_init = 0.8 - 0.6 * math.exp(-0.3 * i)
            slopes = _alibi(DIFF_HEADS).reshape(DIFF_G, DIFF_R)
            wq = diff_wq[j].reshape(d, DIFF_G, DIFF_R, 2, DIFF_HD).transpose(0, 1, 3, 2, 4).reshape(d, -1)
            wq = _pad_heads(wq, 2 * DIFF_HEADS, DIFF_HD, DIFF_HD ** -0.5)
            qrow = _slope_aug(np.broadcast_to(slopes[:, None, :], (DIFF_G, 2, DIFF_R)).reshape(-1))
            wk2 = _pad_heads(diff_wk[j], 2 * DIFF_G, DIFF_HD)
            wkv = jnp.concatenate([diff_wk[j], diff_wv[j]], axis=1).astype(BF16)
            c = wkv.shape[1] // 2
            lam4 = jnp.stack([diff_lq1[j], diff_lk1[j], diff_lq2[j], diff_lk2[j]])
            wo = diff_wo[j].astype(BF16)
            bo = zrow(d)

            qp, kp, kT, v, _, vTb = _proj(hp, gmix, wq, qrow, wk2, zrow(wk2.shape[1]), _pos_aug(pos_p), wkv, zrow(2 * c),
                                          batch=bsz, tk=tq)
            a_p = _flash_diff(qp, kp, vTb, lam4, diff_subln[j].reshape(-1, 1), batch=bsz, tq=tq, lam_init=lam_init)
            outs["dkp"].append(jnp.transpose(kT.reshape(bsz, DIFF_G, 2, DIFF_HD, seq), (0, 4, 1, 2, 3)))
            outs["dvp"].append(v.reshape(bsz, seq, DIFF_G, 2 * DIFF_HD))

            qs, _, kTs, vs, _, _ = _proj(hs, gmix, wq, qrow, wk2, zrow(wk2.shape[1]), _pos_aug(pos_s), wkv, zrow(2 * c),
                                         batch=1, tk=ms)
            k_s = kTs[0].T.reshape(t_s, nb, DIFF_G, 2, DIFF_HD)
            v_s = vs.reshape(t_s, nb, DIFF_G, 2 * DIFF_HD)
            q6 = qs[:, :, :DIFF_HD].reshape(2 * DIFF_G, DIFF_R, t_s, nb, DIFF_HD)
            eye = jnp.eye(2 * DIFF_G, dtype=BF16)
            qbd = jnp.einsum("xrtbd,xy->bxrtyd", q6, eye).reshape(nb, 2 * DIFF_G * DIFF_R * t_s, 2 * DIFF_G * DIFF_HD)
            kn = jnp.pad(jnp.transpose(k_s, (1, 0, 2, 3, 4)).reshape(nb, t_s, -1), ((0, 0), (0, 8 - t_s), (0, 0)))
            vn = jnp.pad(jnp.transpose(v_s, (1, 2, 0, 3)), ((0, 0), (0, 0), (0, 8 - t_s), (0, 0)))
            slope_rows = np.broadcast_to(slopes[:, None, :, None], (DIFF_G, 2, DIFF_R, t_s)).reshape(-1, 1)
            o_s = _decode_diff(page_table, kc, vc, qbd, kn, vn, jnp.asarray(slope_rows), lam4,
                               diff_subln[j].reshape(1, -1), layer=j, lam_init=lam_init)
            a_s = jnp.transpose(o_s.reshape(nb, DIFF_G, DIFF_R, t_s, 2 * DIFF_HD), (3, 0, 1, 2, 4)).reshape(ms, -1).astype(BF16)
            outs["dks"].append(jnp.transpose(k_s, (1, 0, 2, 3, 4)))
            outs["dvs"].append(jnp.transpose(v_s, (1, 0, 2, 3)))
        elif kind == 1:
            scale = (MLA_NOPE + MLA_ROPE) ** -0.5
            hq = MLA_NOPE + MLA_ROPE
            r = mla_wdkv.shape[2] - MLA_ROPE
            wuq = mla_wuq[j].reshape(-1, MLA_HEADS, hq) * scale
            w_rope = wuq[:, :, MLA_NOPE:]
            wdkv_r = mla_wdkv[j][:, r:]
            wukT = jnp.transpose(mla_wuk[j], (1, 2, 0))
            z = jnp.zeros_like(wukT[0::2])
            wukbd = jnp.concatenate([jnp.concatenate([wukT[0::2], z], axis=2),
                                     jnp.concatenate([z, wukT[1::2]], axis=2)], axis=1).astype(BF16)
            w = dict(
                wdq=mla_wdq[j].astype(BF16), qn=mla_q_norm[j].reshape(1, -1),
                wn=wuq[:, :, :MLA_NOPE].reshape(wuq.shape[0], -1).astype(BF16),
                wr=_pad_heads(w_rope.reshape(wuq.shape[0], -1), MLA_HEADS, MLA_ROPE),
                wrs=_pad_heads(_swap_pairs(w_rope).reshape(wuq.shape[0], -1), MLA_HEADS, MLA_ROPE),
                wukbd=wukbd, wc=mla_wdkv[j][:, :r].astype(BF16), kvn=mla_kv_norm[j].reshape(1, -1),
                wkr=_pad_heads(wdkv_r, 1, MLA_ROPE), wkrs=_pad_heads(_swap_pairs(wdkv_r), 1, MLA_ROPE))
            wuvT = jnp.transpose(mla_wuv[j], (1, 2, 0)).astype(BF16)
            wo = mla_wo[j].astype(BF16)
            bo = zrow(d)

            cos_p, sin_p = _rope_tables(pos_p)
            qp, kp, ckv, kpeT, ckvTb = _proj_mla(hp, gmix, w, cos_p, sin_p, batch=bsz, tk=tq)
            a_p = _flash_mla(qp, kp, ckvTb, wuvT, batch=bsz, tq=tq)
            outs["mcp"].append(ckv.reshape(bsz, seq, r))
            outs["mpp"].append(jnp.transpose(kpeT, (0, 2, 1)))

            cos_s, sin_s = _rope_tables(pos_s)
            qs, _, ckv_s, kpeT_s, _ = _proj_mla(hs, gmix, w, cos_s, sin_s, batch=1, tk=ms)
            kpe_s = kpeT_s[0].T.reshape(t_s, nb, MLA_ROPE)
            ckv_s = ckv_s.reshape(t_s, nb, r)
            q4 = jnp.transpose(qs.reshape(MLA_HEADS, t_s, nb, 2 * LANES), (2, 1, 0, 3)).reshape(nb, t_s * MLA_HEADS, 2 * LANES)
            pad8 = ((0, 0), (0, 8 - t_s), (0, 0))
            o_s = _decode_mla(page_table, cache_mla_ckv, pc, q4[:, :, :r], q4[:, :, LANES:LANES + MLA_ROPE],
                              jnp.pad(jnp.transpose(ckv_s, (1, 0, 2)), pad8), jnp.pad(jnp.transpose(kpe_s, (1, 0, 2)), pad8),
                              layer=j, t_s=t_s)
            o_h = jnp.transpose(o_s.reshape(nb, t_s, MLA_HEADS, r), (2, 1, 0, 3)).reshape(MLA_HEADS, ms, r)
            a_s = jnp.transpose(_uv_sample(o_h, jnp.transpose(wuvT, (0, 2, 1))), (1, 0, 2)).reshape(ms, -1)
            outs["mcs"].append(jnp.transpose(ckv_s, (1, 0, 2)))
            outs["mps"].append(jnp.transpose(kpe_s, (1, 0, 2)))
        else:
            nq_, nk_ = SWA_HEADS * SWA_HD, SWA_G * SWA_HD
            slopes = _alibi(SWA_HEADS).reshape(SWA_G, SWA_R)
            wqkv, bqkv = swa_wqkv[j], swa_bqkv[j]
            sc = SWA_HD ** -0.5
            wq = _pad_heads(wqkv[:, :nq_], SWA_HEADS, SWA_HD, sc)
            qrow = jnp.pad((bqkv[:nq_] * sc).reshape(SWA_HEADS, SWA_HD), ((0, 0), (0, LANES - SWA_HD))).reshape(1, -1)
            wk2 = _pad_heads(wqkv[:, nq_:nq_ + nk_], SWA_G, SWA_HD)
            krow = jnp.pad(bqkv[nq_:nq_ + nk_].reshape(SWA_G, SWA_HD), ((0, 0), (0, LANES - SWA_HD))).reshape(1, -1)
            wkv = wqkv[:, nq_:].astype(BF16)
            kvrow = bqkv[nq_:].reshape(1, -1)
            wo = swa_wo[j].astype(BF16)
            bo = swa_bo[j].reshape(1, d)
            sinks = swa_sinks[j].reshape(SWA_G, SWA_R)
            w_ = WINDOW

            dist = (w_ + np.arange(w_)[None, :] - np.arange(2 * w_)[:, None]).astype(np.float32)
            valid = (dist >= 0) & (dist < w_)
            bias = np.where(valid[None, None], -slopes[:, :, None, None] * dist[None, None], NEG)
            bias = np.transpose(bias, (0, 2, 1, 3)).reshape(SWA_G, 2 * w_, SWA_R * w_).astype(np.float32)
            sink_p = jnp.broadcast_to(sinks[:, :, None], (SWA_G, SWA_R, w_)).reshape(SWA_G, 1, SWA_R * w_)
            kaug0 = jnp.zeros((mp, LANES), F32)
            qp, kp, kT, _, vT, vTb = _proj(hp, gmix, wq, qrow, wk2, krow, kaug0, wkv, kvrow, batch=bsz, tk=w_)
            a_p = _swa_prompt(qp, kp, vTb, jnp.asarray(bias), sink_p, batch=bsz)
            tail = lambda xT: jnp.transpose(xT[:, :, seq - w_:].reshape(bsz, SWA_G, SWA_HD, w_), (0, 3, 1, 2))
            outs["skp"].append(tail(kT))
            outs["svp"].append(tail(vT))

            qs, _, kTs, _, vTs, _ = _proj(hs, gmix, wq, qrow, wk2, krow, jnp.zeros((ms, LANES), F32), wkv, kvrow,
                                          batch=1, tk=ms)
            k_s = kTs[0].T.reshape(t_s, nb, SWA_G, SWA_HD)
            v_s = vTs[0].T.reshape(t_s, nb, SWA_G, SWA_HD)
            q5 = qs[:, :, :SWA_HD].reshape(SWA_G, SWA_R, t_s, nb, SWA_HD)
            qbd = jnp.einsum("grtbd,gy->bgrtyd", q5, jnp.eye(SWA_G, dtype=BF16)).reshape(nb, SWA_HEADS * t_s, nk_)
            stT = lambda st: jnp.transpose(st, (0, 2, 3, 1)).reshape(nb, nk_, w_)
            pad8 = ((0, 0), (0, 8 - t_s), (0, 0))
            kn = jnp.pad(jnp.transpose(k_s, (1, 0, 2, 3)).reshape(nb, t_s, nk_), pad8)
            vn = jnp.pad(jnp.transpose(v_s, (1, 0, 2, 3)).reshape(nb, t_s, nk_), pad8)
            tt = np.arange(t_s)
            d_old = (tt[:, None] + w_ - np.arange(w_)[None, :]).astype(np.float32)
            d_new = (tt[:, None] - np.arange(8)[None, :]).astype(np.float32)
            mk = lambda dd, ok: np.where(ok[None, None], -slopes[:, :, None, None] * dd[None, None], NEG).reshape(
                SWA_HEADS * t_s, -1).astype(np.float32)
            bold = mk(d_old, d_old < w_)
            bnew = mk(d_new, (d_new >= 0) & (np.arange(8)[None, :] < t_s))
            sink_s = jnp.broadcast_to(sinks[:, :, None], (SWA_G, SWA_R, t_s)).reshape(-1, 1)
            o_s = _swa_decode(qbd, stT(state_swa_k[j]), stT(state_swa_v[j]), kn, vn, jnp.asarray(bold), jnp.asarray(bnew), sink_s)
            o5 = o_s.reshape(nb, SWA_G, SWA_R, t_s, SWA_G, SWA_HD)
            o_g = jnp.stack([o5[:, g, :, :, g] for g in range(SWA_G)], axis=1)
            a_s = jnp.transpose(o_g, (3, 0, 1, 2, 4)).reshape(ms, -1).astype(BF16)
            k_bt = jnp.transpose(k_s, (1, 0, 2, 3))
            v_bt = jnp.transpose(v_s, (1, 0, 2, 3))
            outs["sks"].append(jnp.concatenate([state_swa_k[j], k_bt], axis=1)[:, -w_:])
            outs["svs"].append(jnp.concatenate([state_swa_v[j], v_bt], axis=1)[:, -w_:])

        final = i == depth - 1
        gffn = norm_ffn[i].reshape(1, d)
        gfin = norm_final.reshape(1, d)
        win = ffn_w_in[i].astype(BF16)
        wout = ffn_w_out[i].astype(BF16)
        cw = ffn_conv_w[i]
        cb = ffn_conv_b[i].reshape(1, dff)
        hp, cst = _ffn(hp, a_p, wo, bo, gffn, win, cw, cb, wout, gfin, jnp.zeros((8, LANES), F32),
                       time_major=False, seq=seq, final=final)
        tps = cst.shape[0] // bsz
        outs["cvp"].append(cst.reshape(bsz, tps, 8, dff)[:, -1, 8 - (CONV_W - 1):])
        prev = jnp.swapaxes(state_ffn_conv[i], 0, 1).reshape((CONV_W - 1) * nb, dff)
        hs, cst_s = _ffn(hs, a_s, wo, bo, gffn, win, cw, cb, wout, gfin, prev, time_major=True, seq=t_s, final=final)
        outs["cvs"].append(jnp.swapaxes(cst_s.reshape(CONV_W - 1, nb, dff), 0, 1))

    y_prompt = hp.reshape(bsz, seq, d)
    y_sample = jnp.swapaxes(hs.reshape(t_s, nb, d), 0, 1)
    st = lambda k: jnp.stack(outs[k])
    return (y_prompt, y_sample, st("dkp"), st("dvp"), st("dks"), st("dvs"), st("mcp"), st("mpp"), st("mcs"), st("mps"),
            st("skp"), st("svp"), st("sks"), st("svs"), st("cvp"), st("cvs"))
```

```python
import functools
import math

import numpy as np
import jax
import jax.numpy as jnp
from jax import lax
from jax.experimental import pallas as pl
from jax.experimental.pallas import tpu as pltpu

F32 = jnp.float32
BF16 = jnp.bfloat16
NEG = -1e30
LANES = 128
VMEM_LIMIT = 56 * 1024 * 1024

N_MIXERS = 3
DIFF_HEADS, DIFF_G, DIFF_HD = 8, 2, 64
DIFF_R = DIFF_HEADS // DIFF_G
DIFF_SUBLN_EPS = 1e-5
MLA_HEADS, MLA_NOPE, MLA_ROPE, MLA_V = 16, 64, 32, 64
ROPE_THETA = 10000.0
SWA_HEADS, SWA_G, SWA_HD, WINDOW = 16, 2, 64, 128
SWA_R = SWA_HEADS // SWA_G
CONV_W = 3
PAGE = 128
EPS = 1e-6
POS_SPLIT = 64
FLASH_TILE = 512
PAGE_SLOTS = 4
LOG2E = math.log2(math.e)


def _dot(a, b):
    return jnp.dot(a, b, preferred_element_type=F32)


def _dot_nt(a, b):
    return lax.dot_general(a, b, (((1,), (1,)), ((), ())), preferred_element_type=F32)


def _rms_rows(x, g, eps):
    return x * lax.rsqrt(jnp.mean(x * x, axis=-1, keepdims=True) + eps) * g


def _params(sem):
    return pltpu.CompilerParams(dimension_semantics=sem, vmem_limit_bytes=VMEM_LIMIT)


def _const_spec(shape):
    nd = len(shape)
    return pl.BlockSpec(shape, lambda *_: (0,) * nd, pipeline_mode=pl.Buffered(1))


def _proj_body(x_ref, g_ref, wqT_ref, qcol_ref, wk_ref, krow_ref, kaug_ref, wkvT_ref, kvcol_ref, wv_ref, vrow_ref,
               qT_ref, kp_ref, kT_ref, v_ref, vT32_ref, vTb_ref, *, tk):
    tm = x_ref.shape[0]
    c = kT_ref.shape[1]
    xn = _rms_rows(x_ref[...], g_ref[...], EPS).astype(BF16)
    qT = _dot_nt(wqT_ref[...], xn) + qcol_ref[...]
    for h in range(qT_ref.shape[0]):
        qT_ref[h] = qT[h * LANES:(h + 1) * LANES, :].astype(BF16)
    k2 = _dot(xn, wk_ref[...]) + krow_ref[...]
    kaug = kaug_ref[...]
    for j in range(kp_ref.shape[0]):
        kp_ref[j] = (k2[:, j * LANES:(j + 1) * LANES] + kaug).astype(BF16)
    kvT = _dot_nt(wkvT_ref[...], xn) + kvcol_ref[...]
    kT_ref[0] = kvT[:c]
    vT = kvT[c:]
    vT32_ref[0] = vT
    vTb = vT.astype(BF16)
    for s in range(tm // tk):
        vTb_ref[s] = vTb[:, s * tk:(s + 1) * tk]
    v_ref[...] = _dot(xn, wv_ref[...]) + vrow_ref[...]


def _proj(x, g, wqT, qcol, wk, krow, kaug, wkvT, kvcol, wv, vrow, *, batch, tk):
    m, d = x.shape
    t = m // batch
    tm = min(512, t)
    nq, nk, c = wqT.shape[0] // LANES, wk.shape[1] // LANES, wv.shape[1]
    tps = t // tm
    row = lambda i: (i, 0)
    consts = (wqT, qcol, wk, krow)
    consts2 = (wkvT, kvcol, wv, vrow)
    return pl.pallas_call(
        functools.partial(_proj_body, tk=tk),
        grid=(m // tm,),
        in_specs=[pl.BlockSpec((tm, d), row), _const_spec((1, d))] + [_const_spec(a.shape) for a in consts]
                 + [pl.BlockSpec((tm, LANES), row)] + [_const_spec(a.shape) for a in consts2],
        out_specs=[pl.BlockSpec((nq, LANES, tm), lambda i: (0, 0, i)),
                   pl.BlockSpec((nk, tm, LANES), lambda i: (0, i, 0)),
                   pl.BlockSpec((1, c, tm), lambda i: (i // tps, 0, i % tps)),
                   pl.BlockSpec((tm, c), row),
                   pl.BlockSpec((1, c, tm), lambda i: (i // tps, 0, i % tps)),
                   pl.BlockSpec((tm // tk, c, tk), lambda i: (i, 0, 0))],
        out_shape=[jax.ShapeDtypeStruct((nq, LANES, m), BF16),
                   jax.ShapeDtypeStruct((nk, m, LANES), BF16),
                   jax.ShapeDtypeStruct((batch, c, t), F32),
                   jax.ShapeDtypeStruct((m, c), F32),
                   jax.ShapeDtypeStruct((batch, c, t), F32),
                   jax.ShapeDtypeStruct((m // tk, c, tk), BF16)],
        compiler_params=_params(("arbitrary",)),
        name="proj",
    )(x, g, *consts, kaug, *consts2)


def _flash_chains(qs, k_refs, vT_refs, i, tq, acc_ref, m_ref, l_ref):
    n = len(qs)
    d0 = pl.ds(pl.multiple_of(i * tq, tq), tq)
    for c in range(n):
        s = _dot(k_refs[c][d0, :], qs[c])
        krow = lax.broadcasted_iota(jnp.int32, s.shape, 0)
        qcol = lax.broadcasted_iota(jnp.int32, s.shape, 1) & (tq - 1)
        s = jnp.where(krow <= qcol, s, NEG)
        m0 = jnp.max(s, axis=0, keepdims=True)
        p = jnp.exp2(s - m0)
        m_ref[c] = m0
        l_ref[c] = jnp.sum(p, axis=0, keepdims=True)
        acc_ref[c] = _dot(vT_refs[c][i], p.astype(BF16))

    def step(j, carry):
        dj = pl.ds(pl.multiple_of(j * tq, tq), tq)
        for c in range(n):
            sj = _dot(k_refs[c][dj, :], qs[c])
            m_old = m_ref[c]
            m_new = jnp.maximum(m_old, jnp.max(sj, axis=0, keepdims=True))
            alpha = jnp.exp2(m_old - m_new)
            pj = jnp.exp2(sj - m_new)
            l_ref[c] = alpha * l_ref[c] + jnp.sum(pj, axis=0, keepdims=True)
            acc_ref[c] = alpha * acc_ref[c] + _dot(vT_refs[c][j], pj.astype(BF16))
            m_ref[c] = m_new
        return carry

    lax.fori_loop(0, i, step, 0)
    return [acc_ref[c] / l_ref[c] for c in range(n)]


def _diff_lambda(lam4, lam_init):
    a = jnp.sum(lam4[0:1] * lam4[1:2], axis=-1, keepdims=True)
    b = jnp.sum(lam4[2:3] * lam4[3:4], axis=-1, keepdims=True)
    return jnp.exp(a) - jnp.exp(b) + lam_init


def _flash_diff_body(qT_ref, kp_ref, vT_ref, lam4_ref, subln_ref, out_ref, acc_ref, m_ref, l_ref,
                     *, tq, lam_init):
    i = pl.program_id(2)
    r = DIFF_R
    qs = [jnp.concatenate([qT_ref[mm * r + rr] for rr in range(r)], axis=1) for mm in range(2)]
    o = _flash_chains(qs, [kp_ref.at[0], kp_ref.at[1]], [vT_ref, vT_ref], i, tq, acc_ref, m_ref, l_ref)
    lam = _diff_lambda(lam4_ref[...], lam_init)
    od = o[0] - lam * o[1]
    ms = jnp.mean(od * od, axis=0, keepdims=True)
    od = od * lax.rsqrt(ms + DIFF_SUBLN_EPS) * subln_ref[...] * (1.0 - lam_init)
    for rr in range(r):
        out_ref[:, rr * LANES:(rr + 1) * LANES] = od[:, rr * tq:(rr + 1) * tq].T.astype(out_ref.dtype)


def _flash_diff(qT, kp, vTb, lam4, subln_col, *, batch, tq, lam_init):
    nh, _, m = qT.shape
    t = m // batch
    nq = t // tq
    r = DIFF_R
    return pl.pallas_call(
        functools.partial(_flash_diff_body, tq=tq, lam_init=lam_init),
        grid=(batch, DIFF_G, nq),
        in_specs=[pl.BlockSpec((2 * r, LANES, tq), lambda b, g, i: (g, 0, b * nq + i)),
                  pl.BlockSpec((2, t, LANES), lambda b, g, i: (g, b, 0)),
                  pl.BlockSpec((nq, 2 * DIFF_HD, tq), lambda b, g, i: (b, g, 0)),
                  pl.BlockSpec(lam4.shape, lambda b, g, i: (0, 0)),
                  pl.BlockSpec(subln_col.shape, lambda b, g, i: (0, 0))],
        out_specs=pl.BlockSpec((tq, r * LANES), lambda b, g, i: (b * nq + i, g)),
        out_shape=jax.ShapeDtypeStruct((m, DIFF_G * r * LANES), BF16),
        scratch_shapes=[pltpu.VMEM((2, 2 * DIFF_HD, r * tq), F32),
                        pltpu.VMEM((2, 1, r * tq), F32), pltpu.VMEM((2, 1, r * tq), F32)],
        compiler_params=_params(("arbitrary", "arbitrary", "arbitrary")),
        name="flash_diff",
    )(qT, kp, vTb, lam4, subln_col)


MLA_CHAIN = 4
MLA_GRP = 8


def _flash_mla_body(qT_ref, kp_ref, vT_ref, wuvT_ref, out_ref, acc_ref, m_ref, l_ref, *, tq):
    i = pl.program_id(2)
    nch = MLA_GRP // MLA_CHAIN
    qs = [jnp.concatenate([qT_ref[c * MLA_CHAIN + rr] for rr in range(MLA_CHAIN)], axis=1) for c in range(nch)]
    o = _flash_chains(qs, [kp_ref.at[0]] * nch, [vT_ref] * nch, i, tq, acc_ref, m_ref, l_ref)
    for c in range(nch):
        oT = o[c].astype(BF16)
        for pr in range(MLA_CHAIN // 2):
            h0 = c * MLA_CHAIN + 2 * pr
            parts = [_dot(wuvT_ref[h0 + e], oT[:, (2 * pr + e) * tq:(2 * pr + e + 1) * tq]) for e in range(2)]
            blk = jnp.concatenate(parts, axis=0)
            out_ref[:, (h0 // 2) * LANES:(h0 // 2 + 1) * LANES] = blk.T.astype(out_ref.dtype)


def _flash_mla(qT, kp, ckvTb, wuvT, *, batch, tq):
    nh, dk, m = qT.shape
    t = m // batch
    nq = t // tq
    r = ckvTb.shape[1]
    return pl.pallas_call(
        functools.partial(_flash_mla_body, tq=tq),
        grid=(batch, nh // MLA_GRP, nq),
        in_specs=[pl.BlockSpec((MLA_GRP, dk, tq), lambda b, g, i: (g, 0, b * nq + i)),
                  pl.BlockSpec((1, t, dk), lambda b, g, i: (0, b, 0)),
                  pl.BlockSpec((nq, r, tq), lambda b, g, i: (b, 0, 0)),
                  pl.BlockSpec((MLA_GRP, MLA_V, r), lambda b, g, i: (g, 0, 0))],
        out_specs=pl.BlockSpec((tq, MLA_GRP * MLA_V), lambda b, g, i: (b * nq + i, g)),
        out_shape=jax.ShapeDtypeStruct((m, nh * MLA_V), BF16),
        scratch_shapes=[pltpu.VMEM((MLA_GRP // MLA_CHAIN, r, MLA_CHAIN * tq), F32),
                        pltpu.VMEM((MLA_GRP // MLA_CHAIN, 1, MLA_CHAIN * tq), F32),
                        pltpu.VMEM((MLA_GRP // MLA_CHAIN, 1, MLA_CHAIN * tq), F32)],
        compiler_params=_params(("arbitrary", "arbitrary", "arbitrary")),
        name="flash_mla",
    )(qT, kp, ckvTb, wuvT)


def _proj_mla_body(x_ref, g_ref, wdq_ref, qn_ref, wnT_ref, wrT_ref, wrsT_ref, wukbdT_ref,
                   wc_ref, kvn_ref, wkr_ref, wkrs_ref, cos_ref, sin_ref, cosT_ref, sinT_ref,
                   qT_ref, kp_ref, ckv_ref, kpeT_ref, ckvTb_ref, *, tk):
    tm = x_ref.shape[0]
    cos, sin = cos_ref[...], sin_ref[...]
    cosT, sinT = cosT_ref[...], sinT_ref[...]
    xn = _rms_rows(x_ref[...], g_ref[...], EPS).astype(BF16)
    cq = _rms_rows(_dot(xn, wdq_ref[...]), qn_ref[...], EPS).astype(BF16)
    qnT = _dot_nt(wnT_ref[...], cq).astype(BF16)
    qrT = _dot_nt(wrT_ref[...], cq)
    qrsT = _dot_nt(wrsT_ref[...], cq)
    for pr in range(wukbdT_ref.shape[0]):
        qlT = _dot(wukbdT_ref[pr], qnT[pr * LANES:(pr + 1) * LANES, :])
        for e in range(2):
            h = 2 * pr + e
            hs = slice(h * LANES, (h + 1) * LANES)
            qT_ref[h, 0:LANES, :] = qlT[e * LANES:(e + 1) * LANES, :].astype(BF16)
            qT_ref[h, LANES:2 * LANES, :] = (qrT[hs, :] * cosT + qrsT[hs, :] * sinT).astype(BF16)
    ckv = _rms_rows(_dot(xn, wc_ref[...]), kvn_ref[...], EPS)
    kr = _dot(xn, wkr_ref[...]) * cos + _dot(xn, wkrs_ref[...]) * sin
    ckv_ref[...] = ckv
    kp_ref[0, :, 0:LANES] = ckv.astype(BF16)
    kp_ref[0, :, LANES:2 * LANES] = kr.astype(BF16)
    kpeT_ref[0] = kr.T[0:MLA_ROPE, :]
    ckvT = ckv.T.astype(BF16)
    for s in range(tm // tk):
        ckvTb_ref[s] = ckvT[:, s * tk:(s + 1) * tk]


def _proj_mla(x, g, w, cos, sin, *, batch, tk):
    m, d = x.shape
    t = m // batch
    tm = min(512, t)
    tps = t // tm
    row = lambda i: (i, 0)
    r = w["wc"].shape[1]
    consts = [w["wdq"], w["qn"], w["wnT"], w["wrT"], w["wrsT"], w["wukbdT"], w["wc"], w["kvn"], w["wkr"], w["wkrs"]]
    col = lambda i: (0, i)
    return pl.pallas_call(
        functools.partial(_proj_mla_body, tk=tk),
        grid=(m // tm,),
        in_specs=[pl.BlockSpec((tm, d), row), _const_spec((1, d))] + [_const_spec(a.shape) for a in consts]
                 + [pl.BlockSpec((tm, LANES), row), pl.BlockSpec((tm, LANES), row),
                    pl.BlockSpec((LANES, tm), col), pl.BlockSpec((LANES, tm), col)],
        out_specs=[pl.BlockSpec((MLA_HEADS, 2 * LANES, tm), lambda i: (0, 0, i)),
                   pl.BlockSpec((1, tm, 2 * LANES), lambda i: (0, i, 0)),
                   pl.BlockSpec((tm, r), row),
                   pl.BlockSpec((1, MLA_ROPE, tm), lambda i: (i // tps, 0, i % tps)),
                   pl.BlockSpec((tm // tk, r, tk), lambda i: (i, 0, 0))],
        out_shape=[jax.ShapeDtypeStruct((MLA_HEADS, 2 * LANES, m), BF16),
                   jax.ShapeDtypeStruct((1, m, 2 * LANES), BF16),
                   jax.ShapeDtypeStruct((m, r), F32),
                   jax.ShapeDtypeStruct((batch, MLA_ROPE, t), F32),
                   jax.ShapeDtypeStruct((m // tk, r, tk), BF16)],
        compiler_params=_params(("arbitrary",)),
        name="proj_mla",
    )(x, g, *consts, cos, sin, cos.T, sin.T)


def _uv_body(o_ref, wuv_ref, out_ref):
    for h in range(o_ref.shape[0]):
        out_ref[h] = _dot(o_ref[h].astype(BF16), wuv_ref[h]).astype(out_ref.dtype)


def _uv_sample(o_lat, wuv):
    h, m, _ = o_lat.shape
    return pl.pallas_call(
        _uv_body,
        out_shape=jax.ShapeDtypeStruct((h, m, wuv.shape[2]), BF16),
        compiler_params=pltpu.CompilerParams(vmem_limit_bytes=VMEM_LIMIT),
        name="uv_sample",
    )(o_lat, wuv)


def _ffn_body(h_ref, a_ref, wo_ref, bo_ref, g_ref, win_ref, cw_ref, cb_ref, wout_ref, gfin_ref, prev_ref,
              out_ref, cst_ref, gs_ref, carry_ref, *, time_major, tiles_per_seq, final, fc):
    i = pl.program_id(0)
    tm = h_ref.shape[0]
    dff = wout_ref.shape[0]
    h1 = h_ref[...] + _dot(a_ref[...], wo_ref[...]) + bo_ref[...]
    xn = _rms_rows(h1, g_ref[...], EPS).astype(BF16)
    if not time_major:
        @pl.when(i % tiles_per_seq == 0)
        def _():
            carry_ref[...] = jnp.zeros(carry_ref.shape, F32)
    acc = h1
    for c in range(dff // fc):
        cs = slice(c * fc, (c + 1) * fc)
        gch = _dot(xn, win_ref[:, cs])
        uch = _dot(xn, win_ref[:, dff + c * fc:dff + (c + 1) * fc])
        cw = cw_ref[:, cs]
        if time_major:
            nb = prev_ref.shape[0] // (CONV_W - 1)
            gs_ref[0:2 * nb, :] = prev_ref[:, cs]
            gs_ref[2 * nb:2 * nb + tm, :] = gch
            gc = cb_ref[:, cs] + cw[0:1] * gs_ref[0:tm, :] + cw[1:2] * gs_ref[nb:nb + tm, :] + cw[2:3] * gch
            cst_ref[:, cs] = gch[tm - 2 * nb:tm, :]
        else:
            gs_ref[0:8, :] = carry_ref[:, cs]
            gs_ref[8:8 + tm, :] = gch
            gc = cb_ref[:, cs] + cw[0:1] * gs_ref[6:6 + tm, :] + cw[1:2] * gs_ref[7:7 + tm, :] + cw[2:3] * gch
            carry_ref[:, cs] = gch[tm - 8:tm, :]
        hid = (jax.nn.gelu(gc) * uch).astype(BF16)
        acc = acc + _dot(hid, wout_ref[cs, :])
    if not time_major:
        cst_ref[0] = carry_ref[...]
    if final:
        acc = _rms_rows(acc, gfin_ref[...], EPS)
    out_ref[...] = acc


def _ffn(h, a, wo, bo, g, win, cw, cb, wout, gfin, prev, *, time_major, seq, final):
    m, d = h.shape
    dff = wout.shape[0]
    fc = 256
    if time_major:
        tm, tiles_per_seq = m, 1
        cst_shape = (prev.shape[0], dff)
        cst_spec = pl.BlockSpec(cst_shape, lambda i: (0, 0))
        gs_rows = prev.shape[0] + tm
    else:
        tm = min(512, seq)
        tiles_per_seq = seq // tm
        cst_shape = (m // tm, 8, dff)
        cst_spec = pl.BlockSpec((1, 8, dff), lambda i: (i, 0, 0))
        gs_rows = 8 + tm
    row = lambda i: (i, 0)
    return pl.pallas_call(
        functools.partial(_ffn_body, time_major=time_major, tiles_per_seq=tiles_per_seq, final=final, fc=fc),
        grid=(m // tm,),
        in_specs=[pl.BlockSpec((tm, d), row), pl.BlockSpec((tm, a.shape[1]), row),
                  _const_spec(wo.shape), _const_spec(bo.shape), _const_spec(g.shape),
                  _const_spec(win.shape), _const_spec(cw.shape), _const_spec(cb.shape),
                  _const_spec(wout.shape), _const_spec(gfin.shape), _const_spec(prev.shape)],
        out_specs=[pl.BlockSpec((tm, d), row), cst_spec],
        out_shape=[jax.ShapeDtypeStruct((m, d), F32), jax.ShapeDtypeStruct(cst_shape, F32)],
        scratch_shapes=[pltpu.VMEM((gs_rows, fc), F32), pltpu.VMEM((8, dff), F32)],
        compiler_params=_params(("arbitrary",)),
        name="ffn_sample" if time_major else "ffn",
    )(h, a, wo, bo, g, win, cw, cb, wout, gfin, prev)


def _swa_prompt_body(qT_ref, kprev_ref, kcur_ref, vprev_ref, vcur_ref, bias_ref, sink_ref, out_ref):
    i = pl.program_id(1)
    w = kcur_ref.shape[1]
    r = SWA_R
    for g in range(SWA_G):
        q = jnp.concatenate([qT_ref[g * r + rr] for rr in range(r)], axis=1)
        kcat = jnp.concatenate([kprev_ref[g], kcur_ref[g]], axis=0)
        s = _dot(kcat, q) + bias_ref[g]
        krow = lax.broadcasted_iota(jnp.int32, s.shape, 0)
        s = jnp.where(jnp.logical_and(i == 0, krow < w), NEG, s)
        sink = sink_ref[g]
        mx = jnp.maximum(jnp.max(s, axis=0, keepdims=True), sink)
        p = jnp.exp(s - mx)
        l = jnp.sum(p, axis=0, keepdims=True) + jnp.exp(sink - mx)
        hs = slice(g * SWA_HD, (g + 1) * SWA_HD)
        vcat = jnp.concatenate([vprev_ref[hs, :], vcur_ref[hs, :]], axis=1)
        oT = _dot(vcat, p.astype(BF16)) / l
        for pr in range(r // 2):
            blk = jnp.concatenate([oT[:, (2 * pr) * w:(2 * pr + 1) * w], oT[:, (2 * pr + 1) * w:(2 * pr + 2) * w]], axis=0)
            c0 = (g * r + 2 * pr) * SWA_HD
            out_ref[:, c0:c0 + 2 * SWA_HD] = blk.T.astype(out_ref.dtype)


def _swa_prompt(qT, kp, vTb, bias, sink, *, batch):
    nh, _, m = qT.shape
    w = WINDOW
    t = m // batch
    nq = t // w
    c = SWA_G * SWA_HD
    vT2 = jnp.transpose(vTb, (1, 0, 2)).reshape(c, m)
    cur = lambda b, i: (0, b * nq + i, 0)
    prv = lambda b, i: (0, b * nq + jnp.maximum(i - 1, 0), 0)
    return pl.pallas_call(
        _swa_prompt_body,
        grid=(batch, nq),
        in_specs=[pl.BlockSpec((nh, LANES, w), lambda b, i: (0, 0, b * nq + i)),
                  pl.BlockSpec((SWA_G, w, LANES), prv), pl.BlockSpec((SWA_G, w, LANES), cur),
                  pl.BlockSpec((c, w), lambda b, i: (0, b * nq + jnp.maximum(i - 1, 0))),
                  pl.BlockSpec((c, w), lambda b, i: (0, b * nq + i)),
                  pl.BlockSpec(bias.shape, lambda b, i: (0, 0, 0)),
                  pl.BlockSpec(sink.shape, lambda b, i: (0, 0, 0))],
        out_specs=pl.BlockSpec((w, nh * SWA_HD), lambda b, i: (b * nq + i, 0)),
        out_shape=jax.ShapeDtypeStruct((m, nh * SWA_HD), BF16),
        compiler_params=_params(("arbitrary", "arbitrary")),
        name="swa_prompt",
    )(qT, kp, kp, vT2, vT2, bias, sink)


def _swa_decode_body(q_ref, kT_ref, vT_ref, kn_ref, vn_ref, bold_ref, bnew_ref, sink_ref, o_ref):
    for b in range(q_ref.shape[0]):
        q = q_ref[b]
        s_old = _dot(q, kT_ref[b].astype(BF16)) + bold_ref[...]
        s_new = _dot_nt(q, kn_ref[b].astype(BF16)) + bnew_ref[...]
        sink = sink_ref[...]
        mx = jnp.maximum(jnp.maximum(jnp.max(s_old, axis=1, keepdims=True), jnp.max(s_new, axis=1, keepdims=True)), sink)
        p_old = jnp.exp(s_old - mx)
        p_new = jnp.exp(s_new - mx)
        l = jnp.sum(p_old, axis=1, keepdims=True) + jnp.sum(p_new, axis=1, keepdims=True) + jnp.exp(sink - mx)
        o = _dot_nt(p_old.astype(BF16), vT_ref[b].astype(BF16)) + _dot(p_new.astype(BF16), vn_ref[b].astype(BF16))
        o_ref[b] = o / l


def _swa_decode(qbd, kT, vT, kn, vn, bold, bnew, sink):
    nb = qbd.shape[0]
    bb = min(8, nb)
    rows = qbd.shape[1]
    blk = lambda shp: pl.BlockSpec((bb,) + shp, lambda i: (i, 0, 0))
    cst = lambda a: pl.BlockSpec(a.shape, lambda i: (0, 0))
    return pl.pallas_call(
        _swa_decode_body,
        grid=(nb // bb,),
        in_specs=[blk(qbd.shape[1:]), blk(kT.shape[1:]), blk(vT.shape[1:]), blk(kn.shape[1:]), blk(vn.shape[1:]),
                  cst(bold), cst(bnew), cst(sink)],
        out_specs=blk((rows, kT.shape[1])),
        out_shape=jax.ShapeDtypeStruct((nb, rows, kT.shape[1]), F32),
        compiler_params=_params(("arbitrary",)),
        name="swa_decode",
    )(qbd, kT, vT, kn, vn, bold, bnew, sink)


def _page_copies(pt_ref, srcs, bufs, sem, layer, gidx, slot, nc, ch):
    b = gidx // nc
    c = gidx % nc
    out = []
    for p in range(ch):
        pg = pt_ref[b, c * ch + p]
        for src, buf in zip(srcs, bufs):
            out.append(pltpu.make_async_copy(src.at[layer, pg], buf.at[slot, p], sem.at[slot]))
    return out


def _decode_diff_body(pt_ref, kc_ref, vc_ref, q_ref, kn_ref, vn_ref, slope_ref, lam4_ref, subln_ref, o_ref,
                      kbuf, vbuf, sem, *, layer, ch, nc, nbuf, nb, past_len, lam_init):
    total = nb * nc
    rows = q_ref.shape[1]
    hg = rows // DIFF_G
    slope = slope_ref[...]
    copies = functools.partial(_page_copies, pt_ref, (kc_ref, vc_ref), (kbuf, vbuf), sem, layer, nc=nc, ch=ch)

    for ahead in range(nbuf - 1):
        for cp in copies(ahead, ahead):
            cp.start()

    def chunk(gidx, slot, c, q, carry):
        m, l, acc = carry

        @pl.when(gidx + nbuf - 1 < total)
        def _():
            for cp in copies(gidx + nbuf - 1, (slot + nbuf - 1) % nbuf):
                cp.start()

        for cp in copies(gidx, slot):
            cp.wait()
        s = jnp.concatenate([_dot(q, kbuf[slot, p].astype(BF16)) for p in range(ch)], axis=1)
        kpos = c * (ch * PAGE) + lax.broadcasted_iota(jnp.int32, (1, ch * PAGE), 1)
        s = s + slope * kpos.astype(F32)
        m_new = jnp.maximum(m, jnp.max(s, axis=1, keepdims=True))
        alpha = jnp.exp2(m - m_new)
        p = jnp.exp2(s - m_new)
        l = alpha * l + jnp.sum(p, axis=1, keepdims=True)
        pb = p.astype(BF16)
        pv = []
        for g in range(DIFF_G):
            vg = vbuf[slot, :, pl.ds(g, PAGE, stride=DIFF_G), :].reshape(ch * PAGE, LANES).astype(BF16)
            pv.append(_dot(pb[g * hg:(g + 1) * hg], vg))
        acc = alpha * acc + jnp.concatenate(pv, axis=0)
        return m_new, l, acc

    def batch_step(b, carry):
        q = q_ref[b]

        def group(cc, st):
            for slot in range(nbuf):
                c = nbuf * cc + slot
                st = chunk(b * nc + c, slot, c, q, st)
            return st

        init = (jnp.full((rows, 1), NEG, F32), jnp.zeros((rows, 1), F32), jnp.zeros((rows, LANES), F32))
        m, l, acc = lax.fori_loop(0, nc // nbuf, group, init)
        s_n = _dot_nt(q, kn_ref[b].astype(BF16))
        tn = lax.broadcasted_iota(jnp.int32, s_n.shape, 1)
        tq = lax.broadcasted_iota(jnp.int32, s_n.shape, 0) & 3
        s_n = jnp.where(tn <= tq, s_n + slope * (past_len + tn).astype(F32), NEG)
        m_new = jnp.maximum(m, jnp.max(s_n, axis=1, keepdims=True))
        alpha = jnp.exp2(m - m_new)
        p_n = jnp.exp2(s_n - m_new)
        l = alpha * l + jnp.sum(p_n, axis=1, keepdims=True)
        pnb = p_n.astype(BF16)
        pv = [_dot(pnb[g * hg:(g + 1) * hg], vn_ref[b, g].astype(BF16)) for g in range(DIFF_G)]
        o = (alpha * acc + jnp.concatenate(pv, axis=0)) / l
        lam = _diff_lambda(lam4_ref[...], lam_init)
        hh = hg // 2
        for g in range(DIFF_G):
            d = o[g * hg:g * hg + hh] - lam * o[g * hg + hh:(g + 1) * hg]
            d = _rms_rows(d, subln_ref[...], DIFF_SUBLN_EPS) * (1.0 - lam_init)
            o_ref[b, g] = d
        return carry

    lax.fori_loop(0, nb, batch_step, 0)


def _decode_diff(pt, kc, vc, qbd, kn, vn, slope, lam4, subln, *, layer, lam_init):
    nb, npages = pt.shape
    ch = min(8, npages // 2)
    nc = npages // ch
    nbuf = min(PAGE_SLOTS, nc)
    rows = qbd.shape[1]
    vm = lambda a: pl.BlockSpec(a.shape, lambda i, pt_: (0,) * a.ndim)
    return pl.pallas_call(
        functools.partial(_decode_diff_body, layer=layer, ch=ch, nc=nc, nbuf=nbuf, nb=nb, past_len=npages * PAGE, lam_init=lam_init),
        grid_spec=pltpu.PrefetchScalarGridSpec(
            num_scalar_prefetch=1, grid=(1,),
            in_specs=[pl.BlockSpec(memory_space=pl.ANY), pl.BlockSpec(memory_space=pl.ANY),
                      vm(qbd), vm(kn), vm(vn), vm(slope), vm(lam4), vm(subln)],
            out_specs=pl.BlockSpec((nb, DIFF_G, rows // 4, LANES), lambda i, pt_: (0, 0, 0, 0)),
            scratch_shapes=[pltpu.VMEM((nbuf, ch) + kc.shape[2:], F32), pltpu.VMEM((nbuf, ch) + vc.shape[2:], F32),
                            pltpu.SemaphoreType.DMA((nbuf,))]),
        out_shape=jax.ShapeDtypeStruct((nb, DIFF_G, rows // 4, LANES), F32),
        compiler_params=_params(("arbitrary",)),
        name="decode_diff",
    )(pt, kc, vc, qbd, kn, vn, slope, lam4, subln)


def _decode_mla_body(pt_ref, cc_ref, pc_ref, ql_ref, qp_ref, cn_ref, pn_ref, o_ref, cbuf, pbuf, sem,
                     *, layer, ch, nc, nbuf, nb, t_s):
    total = nb * nc
    rows = ql_ref.shape[1]
    r = cbuf.shape[-1]
    copies = functools.partial(_page_copies, pt_ref, (cc_ref, pc_ref), (cbuf, pbuf), sem, layer, nc=nc, ch=ch)

    for ahead in range(nbuf - 1):
        for cp in copies(ahead, ahead):
            cp.start()

    def chunk(gidx, slot, ql, qpe, carry):
        m, l, acc = carry

        @pl.when(gidx + nbuf - 1 < total)
        def _():
            for cp in copies(gidx + nbuf - 1, (slot + nbuf - 1) % nbuf):
                cp.start()

        for cp in copies(gidx, slot):
            cp.wait()
        ckv = cbuf[slot].reshape(ch * PAGE, r).astype(BF16)
        s = _dot_nt(ql, ckv) + jnp.concatenate([_dot(qpe, pbuf[slot, p].astype(BF16)) for p in range(ch)], axis=1)
        m_new = jnp.maximum(m, jnp.max(s, axis=1, keepdims=True))
        alpha = jnp.exp2(m - m_new)
        p = jnp.exp2(s - m_new)
        l = alpha * l + jnp.sum(p, axis=1, keepdims=True)
        acc = alpha * acc + _dot(p.astype(BF16), ckv)
        return m_new, l, acc

    def batch_step(b, carry):
        ql = ql_ref[b]
        qpe = qp_ref[b]

        def group(cc, st):
            for slot in range(nbuf):
                st = chunk(b * nc + nbuf * cc + slot, slot, ql, qpe, st)
            return st

        init = (jnp.full((rows, 1), NEG, F32), jnp.zeros((rows, 1), F32), jnp.zeros((rows, r), F32))
        m, l, acc = lax.fori_loop(0, nc // nbuf, group, init)
        cn = cn_ref[b].astype(BF16)
        s_n = _dot_nt(ql, cn) + _dot_nt(qpe, pn_ref[b].astype(BF16))
        tn = lax.broadcasted_iota(jnp.int32, s_n.shape, 1)
        tq = lax.broadcasted_iota(jnp.int32, s_n.shape, 0) // (rows // t_s)
        s_n = jnp.where(tn <= tq, s_n, NEG)
        m_new = jnp.maximum(m, jnp.max(s_n, axis=1, keepdims=True))
        alpha = jnp.exp2(m - m_new)
        p_n = jnp.exp2(s_n - m_new)
        l = alpha * l + jnp.sum(p_n, axis=1, keepdims=True)
        o_ref[b] = (alpha * acc + _dot(p_n.astype(BF16), cn)) / l
        return carry

    lax.fori_loop(0, nb, batch_step, 0)


def _decode_mla(pt, cc, pc, ql, qpe, cn, pn, *, layer, t_s):
    nb, npages = pt.shape
    ch = min(16, npages // 2)
    nc = npages // ch
    nbuf = min(PAGE_SLOTS, nc)
    rows = ql.shape[1]
    vm = lambda a: pl.BlockSpec(a.shape, lambda i, pt_: (0,) * a.ndim)
    return pl.pallas_call(
        functools.partial(_decode_mla_body, layer=layer, ch=ch, nc=nc, nbuf=nbuf, nb=nb, t_s=t_s),
        grid_spec=pltpu.PrefetchScalarGridSpec(
            num_scalar_prefetch=1, grid=(1,),
            in_specs=[pl.BlockSpec(memory_space=pl.ANY), pl.BlockSpec(memory_space=pl.ANY),
                      vm(ql), vm(qpe), vm(cn), vm(pn)],
            out_specs=pl.BlockSpec((nb, rows, cc.shape[-1]), lambda i, pt_: (0, 0, 0)),
            scratch_shapes=[pltpu.VMEM((nbuf, ch) + cc.shape[2:], F32), pltpu.VMEM((nbuf, ch) + pc.shape[2:], F32),
                            pltpu.SemaphoreType.DMA((nbuf,))]),
        out_shape=jax.ShapeDtypeStruct((nb, rows, cc.shape[-1]), F32),
        compiler_params=_params(("arbitrary",)),
        name="decode_mla",
    )(pt, cc, pc, ql, qpe, cn, pn)


def _alibi(n):
    return np.array([2.0 ** (-8.0 * (h + 1) / n) for h in range(n)], dtype=np.float32)


def _pad_heads(w, nh, hd, scale=1.0):
    d = w.shape[0]
    w3 = (w * scale).reshape(d, nh, hd)
    return jnp.pad(w3, ((0, 0), (0, 0), (0, LANES - hd))).reshape(d, nh * LANES).astype(BF16)


def _pos_aug(pos):
    hi = (pos // POS_SPLIT).astype(F32)
    lo = (pos % POS_SPLIT).astype(F32)
    cols = jnp.stack([hi, hi, lo, lo], axis=1)
    return jnp.pad(cols, ((0, 0), (DIFF_HD, LANES - DIFF_HD - 4)))


def _slope_aug(slopes):
    s = jnp.asarray(slopes, F32)
    hi = s.astype(BF16).astype(F32)
    lo = (s - hi).astype(BF16).astype(F32)
    cols = jnp.stack([hi * POS_SPLIT, lo * POS_SPLIT, hi, lo], axis=1)
    return jnp.pad(cols, ((0, 0), (DIFF_HD, LANES - DIFF_HD - 4))).reshape(1, -1)


def _rope_tables(pos):
    half = MLA_ROPE // 2
    freqs = ROPE_THETA ** (-jnp.arange(half, dtype=F32) * 2.0 / MLA_ROPE)
    ang = pos.astype(F32)[:, None] * freqs[None, :]
    cos = jnp.repeat(jnp.cos(ang), 2, axis=1)
    sin = jnp.stack([-jnp.sin(ang), jnp.sin(ang)], axis=-1).reshape(-1, MLA_ROPE)
    pad = ((0, 0), (0, LANES - MLA_ROPE))
    return jnp.pad(cos, pad), jnp.pad(sin, pad)


def _swap_pairs(w):
    return w.reshape(w.shape[:-1] + (w.shape[-1] // 2, 2))[..., ::-1].reshape(w.shape)


def _tm(x):
    return jnp.swapaxes(x, 0, 1).reshape((x.shape[0] * x.shape[1],) + x.shape[2:])


def kernel(x_prompt, x_sample, cache_diff_k, cache_diff_v, cache_mla_ckv, cache_mla_kpe, state_swa_k, state_swa_v,
           state_ffn_conv, page_table, norm_mix, norm_ffn, norm_final, diff_wq, diff_wk, diff_wv, diff_lq1, diff_lk1,
           diff_lq2, diff_lk2, diff_subln, diff_wo, mla_wdq, mla_q_norm, mla_wuq, mla_wdkv, mla_kv_norm, mla_wuk,
           mla_wuv, mla_wo, swa_wqkv, swa_bqkv, swa_sinks, swa_wo, swa_bo, ffn_w_in, ffn_conv_w, ffn_conv_b, ffn_w_out):
    bsz, seq, d = x_prompt.shape
    nb, t_s, _ = x_sample.shape
    depth = norm_mix.shape[0]
    dff = ffn_w_out.shape[1]
    npages = page_table.shape[1]
    past_len = npages * PAGE
    n_pool = cache_diff_k.shape[1]
    mp, ms = bsz * seq, nb * t_s
    tq = min(FLASH_TILE, seq)

    hp = x_prompt.reshape(mp, d)
    hs = _tm(x_sample)
    pos_p = jnp.tile(jnp.arange(seq, dtype=jnp.int32), bsz)
    pos_s = past_len + jnp.repeat(jnp.arange(t_s, dtype=jnp.int32), nb)
    zrow = lambda n: jnp.zeros((1, n), F32)

    kc = jnp.transpose(cache_diff_k, (0, 1, 3, 4, 5, 2)).reshape(cache_diff_k.shape[0], n_pool, 2 * DIFF_G * DIFF_HD, PAGE)
    vc = cache_diff_v.reshape(cache_diff_v.shape[0], n_pool, PAGE * DIFF_G, 2 * DIFF_HD)
    pc = jnp.transpose(cache_mla_kpe, (0, 1, 3, 2))

    outs = {k: [] for k in ("dkp", "dvp", "dks", "dvs", "mcp", "mpp", "mcs", "mps", "skp", "svp", "sks", "svs", "cvp", "cvs")}
    a_p = a_s = None

    for i in range(depth):
        kind, j = i % N_MIXERS, i // N_MIXERS
        gmix = norm_mix[i].reshape(1, d)
        if kind == 0:
            lam_init = 0.8 - 0.6 * math.exp(-0.3 * i)
            slopes = _alibi(DIFF_HEADS).reshape(DIFF_G, DIFF_R)
            wq = diff_wq[j].reshape(d, DIFF_G, DIFF_R, 2, DIFF_HD).transpose(0, 1, 3, 2, 4).reshape(d, -1)
            wq = _pad_heads(wq, 2 * DIFF_HEADS, DIFF_HD, DIFF_HD ** -0.5 * LOG2E)
            qcol = _slope_aug(np.broadcast_to(slopes[:, None, :], (DIFF_G, 2, DIFF_R)).reshape(-1) * LOG2E).reshape(-1, 1)
            wk2 = _pad_heads(diff_wk[j], 2 * DIFF_G, DIFF_HD)
            wkvT = jnp.concatenate([diff_wk[j], diff_wv[j]], axis=1).T.astype(BF16)
            wv = diff_wv[j].astype(BF16)
            c = wv.shape[1]
            proj_w = (wq.T, qcol, wk2, zrow(wk2.shape[1]))
            proj_w2 = (wkvT, jnp.zeros((2 * c, 1), F32), wv, zrow(c))
            lam4 = jnp.stack([diff_lq1[j], diff_lk1[j], diff_lq2[j], diff_lk2[j]])
            wo = diff_wo[j].astype(BF16)
            bo = zrow(d)

            qT, kp, kT, v, _, vTb = _proj(hp, gmix, *proj_w, _pos_aug(pos_p), *proj_w2, batch=bsz, tk=tq)
            a_p = _flash_diff(qT, kp, vTb, lam4, diff_subln[j].reshape(-1, 1), batch=bsz, tq=tq, lam_init=lam_init)
            outs["dkp"].append(jnp.transpose(kT.reshape(bsz, DIFF_G, 2, DIFF_HD, seq), (0, 4, 1, 2, 3)))
            outs["dvp"].append(v.reshape(bsz, seq, DIFF_G, 2 * DIFF_HD))

            qs, _, kTs, vs, _, _ = _proj(hs, gmix, *proj_w, _pos_aug(pos_s), *proj_w2, batch=1, tk=ms)
            k_s = kTs[0].T.reshape(t_s, nb, DIFF_G, 2, DIFF_HD)
            v_s = vs.reshape(t_s, nb, DIFF_G, 2 * DIFF_HD)
            q6 = qs[:, :DIFF_HD, :].reshape(2 * DIFF_G, DIFF_R, DIFF_HD, t_s, nb)
            eye = jnp.eye(2 * DIFF_G, dtype=BF16)
            qbd = jnp.einsum("xrdtb,xy->bxrtyd", q6, eye).reshape(nb, 2 * DIFF_G * DIFF_R * t_s, 2 * DIFF_G * DIFF_HD)
            kn = jnp.pad(jnp.transpose(k_s, (1, 0, 2, 3, 4)).reshape(nb, t_s, -1), ((0, 0), (0, 8 - t_s), (0, 0)))
            vn = jnp.pad(jnp.transpose(v_s, (1, 2, 0, 3)), ((0, 0), (0, 0), (0, 8 - t_s), (0, 0)))
            slope_rows = (np.broadcast_to(slopes[:, None, :, None], (DIFF_G, 2, DIFF_R, t_s)).reshape(-1, 1) * LOG2E).astype(np.float32)
            o_s = _decode_diff(page_table, kc, vc, qbd, kn, vn, jnp.asarray(slope_rows), lam4,
                               diff_subln[j].reshape(1, -1), layer=j, lam_init=lam_init)
            a_s = jnp.transpose(o_s.reshape(nb, DIFF_G, DIFF_R, t_s, 2 * DIFF_HD), (3, 0, 1, 2, 4)).reshape(ms, -1).astype(BF16)
            outs["dks"].append(jnp.transpose(k_s, (1, 0, 2, 3, 4)))
            outs["dvs"].append(jnp.transpose(v_s, (1, 0, 2, 3)))
        elif kind == 1:
            scale = (MLA_NOPE + MLA_ROPE) ** -0.5 * LOG2E
            hq = MLA_NOPE + MLA_ROPE
            r = mla_wdkv.shape[2] - MLA_ROPE
            wuq = mla_wuq[j].reshape(-1, MLA_HEADS, hq) * scale
            w_rope = wuq[:, :, MLA_NOPE:]
            wdkv_r = mla_wdkv[j][:, r:]
            wukT = jnp.transpose(mla_wuk[j], (1, 2, 0))
            z = jnp.zeros_like(wukT[0::2])
            wukbd = jnp.concatenate([jnp.concatenate([wukT[0::2], z], axis=2),
                                     jnp.concatenate([z, wukT[1::2]], axis=2)], axis=1).astype(BF16)
            w = dict(
                wdq=mla_wdq[j].astype(BF16), qn=mla_q_norm[j].reshape(1, -1),
                wnT=wuq[:, :, :MLA_NOPE].reshape(wuq.shape[0], -1).T.astype(BF16),
                wrT=_pad_heads(w_rope.reshape(wuq.shape[0], -1), MLA_HEADS, MLA_ROPE).T,
                wrsT=_pad_heads(_swap_pairs(w_rope).reshape(wuq.shape[0], -1), MLA_HEADS, MLA_ROPE).T,
                wukbdT=jnp.transpose(wukbd, (0, 2, 1)), wc=mla_wdkv[j][:, :r].astype(BF16), kvn=mla_kv_norm[j].reshape(1, -1),
                wkr=_pad_heads(wdkv_r, 1, MLA_ROPE), wkrs=_pad_heads(_swap_pairs(wdkv_r), 1, MLA_ROPE))
            wuvT = jnp.transpose(mla_wuv[j], (1, 2, 0)).astype(BF16)
            wo = mla_wo[j].astype(BF16)
            bo = zrow(d)

            cos_p, sin_p = _rope_tables(pos_p)
            qT, kp, ckv, kpeT, ckvTb = _proj_mla(hp, gmix, w, cos_p, sin_p, batch=bsz, tk=tq)
            a_p = _flash_mla(qT, kp, ckvTb, wuvT, batch=bsz, tq=tq)
            outs["mcp"].append(ckv.reshape(bsz, seq, r))
            outs["mpp"].append(jnp.transpose(kpeT, (0, 2, 1)))

            cos_s, sin_s = _rope_tables(pos_s)
            qs, _, ckv_s, kpeT_s, _ = _proj_mla(hs, gmix, w, cos_s, sin_s, batch=1, tk=ms)
            kpe_s = kpeT_s[0].T.reshape(t_s, nb, MLA_ROPE)
            ckv_s = ckv_s.reshape(t_s, nb, r)
            q4 = jnp.transpose(qs.reshape(MLA_HEADS, 2 * LANES, t_s, nb), (3, 2, 0, 1)).reshape(nb, t_s * MLA_HEADS, 2 * LANES)
            pad8 = ((0, 0), (0, 8 - t_s), (0, 0))
            o_s = _decode_mla(page_table, cache_mla_ckv, pc, q4[:, :, :r], q4[:, :, LANES:LANES + MLA_ROPE],
                              jnp.pad(jnp.transpose(ckv_s, (1, 0, 2)), pad8), jnp.pad(jnp.transpose(kpe_s, (1, 0, 2)), pad8),
                              layer=j, t_s=t_s)
            o_h = jnp.transpose(o_s.reshape(nb, t_s, MLA_HEADS, r), (2, 1, 0, 3)).reshape(MLA_HEADS, ms, r)
            a_s = jnp.transpose(_uv_sample(o_h, jnp.transpose(wuvT, (0, 2, 1))), (1, 0, 2)).reshape(ms, -1)
            outs["mcs"].append(jnp.transpose(ckv_s, (1, 0, 2)))
            outs["mps"].append(jnp.transpose(kpe_s, (1, 0, 2)))
        else:
            nq_, nk_ = SWA_HEADS * SWA_HD, SWA_G * SWA_HD
            slopes = _alibi(SWA_HEADS).reshape(SWA_G, SWA_R)
            wqkv, bqkv = swa_wqkv[j], swa_bqkv[j]
            sc = SWA_HD ** -0.5
            wq = _pad_heads(wqkv[:, :nq_], SWA_HEADS, SWA_HD, sc)
            qcol = jnp.pad((bqkv[:nq_] * sc).reshape(SWA_HEADS, SWA_HD), ((0, 0), (0, LANES - SWA_HD))).reshape(-1, 1)
            wk2 = _pad_heads(wqkv[:, nq_:nq_ + nk_], SWA_G, SWA_HD)
            krow = jnp.pad(bqkv[nq_:nq_ + nk_].reshape(SWA_G, SWA_HD), ((0, 0), (0, LANES - SWA_HD))).reshape(1, -1)
            proj_w = (wq.T, qcol, wk2, krow)
            proj_w2 = (wqkv[:, nq_:].T.astype(BF16), bqkv[nq_:].reshape(-1, 1), wqkv[:, nq_ + nk_:].astype(BF16),
                       bqkv[nq_ + nk_:].reshape(1, -1))
            wo = swa_wo[j].astype(BF16)
            bo = swa_bo[j].reshape(1, d)
            sinks = swa_sinks[j].reshape(SWA_G, SWA_R)
            w_ = WINDOW

            dist = (w_ + np.arange(w_)[None, :] - np.arange(2 * w_)[:, None]).astype(np.float32)
            valid = (dist >= 0) & (dist < w_)
            bias = np.where(valid[None, None], -slopes[:, :, None, None] * dist[None, None], NEG)
            bias = np.transpose(bias, (0, 2, 1, 3)).reshape(SWA_G, 2 * w_, SWA_R * w_).astype(np.float32)
            sink_p = jnp.broadcast_to(sinks[:, :, None], (SWA_G, SWA_R, w_)).reshape(SWA_G, 1, SWA_R * w_)
            kaug0 = jnp.zeros((mp, LANES), F32)
            qT, kp, kT, _, vT, vTb = _proj(hp, gmix, *proj_w, kaug0, *proj_w2, batch=bsz, tk=w_)
            a_p = _swa_prompt(qT, kp, vTb, jnp.asarray(bias), sink_p, batch=bsz)
            tail = lambda xT: jnp.transpose(xT[:, :, seq - w_:].reshape(bsz, SWA_G, SWA_HD, w_), (0, 3, 1, 2))
            outs["skp"].append(tail(kT))
            outs["svp"].append(tail(vT))

            qs, _, kTs, _, vTs, _ = _proj(hs, gmix, *proj_w, jnp.zeros((ms, LANES), F32), *proj_w2, batch=1, tk=ms)
            k_s = kTs[0].T.reshape(t_s, nb, SWA_G, SWA_HD)
            v_s = vTs[0].T.reshape(t_s, nb, SWA_G, SWA_HD)
            q5 = qs[:, :SWA_HD, :].reshape(SWA_G, SWA_R, SWA_HD, t_s, nb)
            qbd = jnp.einsum("grdtb,gy->bgrtyd", q5, jnp.eye(SWA_G, dtype=BF16)).reshape(nb, SWA_HEADS * t_s, nk_)
            stT = lambda st: jnp.transpose(st, (0, 2, 3, 1)).reshape(nb, nk_, w_)
            pad8 = ((0, 0), (0, 8 - t_s), (0, 0))
            kn = jnp.pad(jnp.transpose(k_s, (1, 0, 2, 3)).reshape(nb, t_s, nk_), pad8)
            vn = jnp.pad(jnp.transpose(v_s, (1, 0, 2, 3)).reshape(nb, t_s, nk_), pad8)
            tt = np.arange(t_s)
            d_old = (tt[:, None] + w_ - np.arange(w_)[None, :]).astype(np.float32)
            d_new = (tt[:, None] - np.arange(8)[None, :]).astype(np.float32)
            mk = lambda dd, ok: np.where(ok[None, None], -slopes[:, :, None, None] * dd[None, None], NEG).reshape(
                SWA_HEADS * t_s, -1).astype(np.float32)
            bold = mk(d_old, d_old < w_)
            bnew = mk(d_new, (d_new >= 0) & (np.arange(8)[None, :] < t_s))
            sink_s = jnp.broadcast_to(sinks[:, :, None], (SWA_G, SWA_R, t_s)).reshape(-1, 1)
            o_s = _swa_decode(qbd, stT(state_swa_k[j]), stT(state_swa_v[j]), kn, vn, jnp.asarray(bold), jnp.asarray(bnew), sink_s)
            o5 = o_s.reshape(nb, SWA_G, SWA_R, t_s, SWA_G, SWA_HD)
            o_g = jnp.stack([o5[:, g, :, :, g] for g in range(SWA_G)], axis=1)
            a_s = jnp.transpose(o_g, (3, 0, 1, 2, 4)).reshape(ms, -1).astype(BF16)
            k_bt = jnp.transpose(k_s, (1, 0, 2, 3))
            v_bt = jnp.transpose(v_s, (1, 0, 2, 3))
            outs["sks"].append(jnp.concatenate([state_swa_k[j], k_bt], axis=1)[:, -w_:])
            outs["svs"].append(jnp.concatenate([state_swa_v[j], v_bt], axis=1)[:, -w_:])

        final = i == depth - 1
        gffn = norm_ffn[i].reshape(1, d)
        gfin = norm_final.reshape(1, d)
        win = ffn_w_in[i].astype(BF16)
        wout = ffn_w_out[i].astype(BF16)
        cw = ffn_conv_w[i]
        cb = ffn_conv_b[i].reshape(1, dff)
        hp, cst = _ffn(hp, a_p, wo, bo, gffn, win, cw, cb, wout, gfin, jnp.zeros((8, LANES), F32),
                       time_major=False, seq=seq, final=final)
        tps = cst.shape[0] // bsz
        outs["cvp"].append(cst.reshape(bsz, tps, 8, dff)[:, -1, 8 - (CONV_W - 1):])
        prev = jnp.swapaxes(state_ffn_conv[i], 0, 1).reshape((CONV_W - 1) * nb, dff)
        hs, cst_s = _ffn(hs, a_s, wo, bo, gffn, win, cw, cb, wout, gfin, prev, time_major=True, seq=t_s, final=final)
        outs["cvs"].append(jnp.swapaxes(cst_s.reshape(CONV_W - 1, nb, dff), 0, 1))

    y_prompt = hp.reshape(bsz, seq, d)
    y_sample = jnp.swapaxes(hs.reshape(t_s, nb, d), 0, 1)
    st = lambda k: jnp.stack(outs[k])
    return (y_prompt, y_sample, st("dkp"), st("dvp"), st("dks"), st("dvs"), st("mcp"), st("mpp"), st("mcs"), st("mps"),
            st("skp"), st("svp"), st("sks"), st("svs"), st("cvp"), st("cvs"))
```

```python
import functools
import math

import numpy as np
import jax
import jax.numpy as jnp
from jax import lax
from jax.experimental import pallas as pl
from jax.experimental.pallas import tpu as pltpu

F32 = jnp.float32
BF16 = jnp.bfloat16
NEG = -1e30
LANES = 128
VMEM_LIMIT = 56 * 1024 * 1024

N_MIXERS = 3
DIFF_HEADS, DIFF_G, DIFF_HD = 8, 2, 64
DIFF_R = DIFF_HEADS // DIFF_G
DIFF_SUBLN_EPS = 1e-5
MLA_HEADS, MLA_NOPE, MLA_ROPE, MLA_V = 16, 64, 32, 64
ROPE_THETA = 10000.0
SWA_HEADS, SWA_G, SWA_HD, WINDOW = 16, 2, 64, 128
SWA_R = SWA_HEADS // SWA_G
CONV_W = 3
PAGE = 128
EPS = 1e-6
POS_SPLIT = 64
FLASH_TILE = 512
PAGE_SLOTS = 4
LOG2E = math.log2(math.e)


def _dot(a, b):
    return jnp.dot(a, b, preferred_element_type=F32)


def _dot_nt(a, b):
    return lax.dot_general(a, b, (((1,), (1,)), ((), ())), preferred_element_type=F32)


def _rms_rows(x, g, eps):
    return x * lax.rsqrt(jnp.mean(x * x, axis=-1, keepdims=True) + eps) * g


def _params(sem):
    return pltpu.CompilerParams(dimension_semantics=sem, vmem_limit_bytes=VMEM_LIMIT)


def _const_spec(shape):
    nd = len(shape)
    return pl.BlockSpec(shape, lambda *_: (0,) * nd, pipeline_mode=pl.Buffered(1))


def _proj_body(x_ref, g_ref, wqT_ref, qcol_ref, wk_ref, krow_ref, kaug_ref, wkvT_ref, kvcol_ref, wv_ref, vrow_ref,
               qT_ref, kp_ref, kT_ref, v_ref, vT32_ref, vTb_ref, *, tk):
    tm = x_ref.shape[0]
    c = kT_ref.shape[1]
    xn = _rms_rows(x_ref[...], g_ref[...], EPS).astype(BF16)
    qT = _dot_nt(wqT_ref[...], xn) + qcol_ref[...]
    for h in range(qT_ref.shape[0]):
        qT_ref[h] = qT[h * LANES:(h + 1) * LANES, :].astype(BF16)
    k2 = _dot(xn, wk_ref[...]) + krow_ref[...]
    kaug = kaug_ref[...]
    for j in range(kp_ref.shape[0]):
        kp_ref[j] = (k2[:, j * LANES:(j + 1) * LANES] + kaug).astype(BF16)
    kvT = _dot_nt(wkvT_ref[...], xn) + kvcol_ref[...]
    kT_ref[0] = kvT[:c]
    vT = kvT[c:]
    vT32_ref[0] = vT
    vTb = vT.astype(BF16)
    for s in range(tm // tk):
        vTb_ref[s] = vTb[:, s * tk:(s + 1) * tk]
    v_ref[...] = _dot(xn, wv_ref[...]) + vrow_ref[...]


def _proj(x, g, wqT, qcol, wk, krow, kaug, wkvT, kvcol, wv, vrow, *, batch, tk):
    m, d = x.shape
    t = m // batch
    tm = min(512, t)
    nq, nk, c = wqT.shape[0] // LANES, wk.shape[1] // LANES, wv.shape[1]
    tps = t // tm
    row = lambda i: (i, 0)
    consts = (wqT, qcol, wk, krow)
    consts2 = (wkvT, kvcol, wv, vrow)
    return pl.pallas_call(
        functools.partial(_proj_body, tk=tk),
        grid=(m // tm,),
        in_specs=[pl.BlockSpec((tm, d), row), _const_spec((1, d))] + [_const_spec(a.shape) for a in consts]
                 + [pl.BlockSpec((tm, LANES), row)] + [_const_spec(a.shape) for a in consts2],
        out_specs=[pl.BlockSpec((nq, LANES, tm), lambda i: (0, 0, i)),
                   pl.BlockSpec((nk, tm, LANES), lambda i: (0, i, 0)),
                   pl.BlockSpec((1, c, tm), lambda i: (i // tps, 0, i % tps)),
                   pl.BlockSpec((tm, c), row),
                   pl.BlockSpec((1, c, tm), lambda i: (i // tps, 0, i % tps)),
                   pl.BlockSpec((tm // tk, c, tk), lambda i: (i, 0, 0))],
        out_shape=[jax.ShapeDtypeStruct((nq, LANES, m), BF16),
                   jax.ShapeDtypeStruct((nk, m, LANES), BF16),
                   jax.ShapeDtypeStruct((batch, c, t), F32),
                   jax.ShapeDtypeStruct((m, c), F32),
                   jax.ShapeDtypeStruct((batch, c, t), F32),
                   jax.ShapeDtypeStruct((m // tk, c, tk), BF16)],
        compiler_params=_params(("arbitrary",)),
        name="proj",
    )(x, g, *consts, kaug, *consts2)


def _flash_chains(qs, k_refs, vT_refs, i, tq, acc_ref, m_ref, l_ref, sa_ref, sb_ref):
    n = len(qs)

    def qk(t, s_ref):
        dt = pl.ds(pl.multiple_of(t * tq, tq), tq)
        for c in range(n):
            s_ref[c] = _dot(k_refs[c][dt, :], qs[c])

    def update(t, s_ref, masked):
        for c in range(n):
            s = s_ref[c]
            if masked:
                krow = lax.broadcasted_iota(jnp.int32, s.shape, 0)
                qcol = lax.broadcasted_iota(jnp.int32, s.shape, 1) & (tq - 1)
                s = jnp.where(krow <= qcol, s, NEG)
            m_old = m_ref[c]
            m_new = jnp.maximum(m_old, jnp.max(s, axis=0, keepdims=True))
            alpha = jnp.exp2(m_old - m_new)
            p = jnp.exp2(s - m_new)
            l_ref[c] = alpha * l_ref[c] + jnp.sum(p, axis=0, keepdims=True)
            acc_ref[c] = alpha * acc_ref[c] + _dot(vT_refs[c][t], p.astype(BF16))
            m_ref[c] = m_new

    m_ref[...] = jnp.full(m_ref.shape, NEG, F32)
    l_ref[...] = jnp.zeros(l_ref.shape, F32)
    acc_ref[...] = jnp.zeros(acc_ref.shape, F32)
    qk(0, sa_ref)

    def pair(jj, carry):
        t = 2 * jj
        qk(t + 1, sb_ref)
        update(t, sa_ref, False)
        qk(t + 2, sa_ref)
        update(t + 1, sb_ref, False)
        return carry

    lax.fori_loop(0, i // 2, pair, 0)

    @pl.when(i % 2 == 0)
    def _():
        update(i, sa_ref, True)

    @pl.when(i % 2 == 1)
    def _():
        qk(i, sb_ref)
        update(i - 1, sa_ref, False)
        update(i, sb_ref, True)

    return [acc_ref[c] / l_ref[c] for c in range(n)]


def _diff_lambda(lam4, lam_init):
    a = jnp.sum(lam4[0:1] * lam4[1:2], axis=-1, keepdims=True)
    b = jnp.sum(lam4[2:3] * lam4[3:4], axis=-1, keepdims=True)
    return jnp.exp(a) - jnp.exp(b) + lam_init


def _flash_diff_body(qT_ref, kp_ref, vT_ref, lam4_ref, subln_ref, out_ref, acc_ref, m_ref, l_ref, sa_ref, sb_ref,
                     *, tq, lam_init):
    i = pl.program_id(2)
    r = DIFF_R
    qs = [jnp.concatenate([qT_ref[mm * r + rr] for rr in range(r)], axis=1) for mm in range(2)]
    o = _flash_chains(qs, [kp_ref.at[0], kp_ref.at[1]], [vT_ref, vT_ref], i, tq, acc_ref, m_ref, l_ref, sa_ref, sb_ref)
    lam = _diff_lambda(lam4_ref[...], lam_init)
    od = o[0] - lam * o[1]
    ms = jnp.mean(od * od, axis=0, keepdims=True)
    od = od * lax.rsqrt(ms + DIFF_SUBLN_EPS) * subln_ref[...] * (1.0 - lam_init)
    for rr in range(r):
        out_ref[:, rr * LANES:(rr + 1) * LANES] = od[:, rr * tq:(rr + 1) * tq].T.astype(out_ref.dtype)


def _flash_diff(qT, kp, vTb, lam4, subln_col, *, batch, tq, lam_init):
    nh, _, m = qT.shape
    t = m // batch
    nq = t // tq
    r = DIFF_R
    return pl.pallas_call(
        functools.partial(_flash_diff_body, tq=tq, lam_init=lam_init),
        grid=(batch, DIFF_G, nq),
        in_specs=[pl.BlockSpec((2 * r, LANES, tq), lambda b, g, i: (g, 0, b * nq + i)),
                  pl.BlockSpec((2, t, LANES), lambda b, g, i: (g, b, 0)),
                  pl.BlockSpec((nq, 2 * DIFF_HD, tq), lambda b, g, i: (b, g, 0)),
                  pl.BlockSpec(lam4.shape, lambda b, g, i: (0, 0)),
                  pl.BlockSpec(subln_col.shape, lambda b, g, i: (0, 0))],
        out_specs=pl.BlockSpec((tq, r * LANES), lambda b, g, i: (b * nq + i, g)),
        out_shape=jax.ShapeDtypeStruct((m, DIFF_G * r * LANES), BF16),
        scratch_shapes=[pltpu.VMEM((2, 2 * DIFF_HD, r * tq), F32),
                        pltpu.VMEM((2, 1, r * tq), F32), pltpu.VMEM((2, 1, r * tq), F32),
                        pltpu.VMEM((2, tq, r * tq), F32), pltpu.VMEM((2, tq, r * tq), F32)],
        compiler_params=_params(("arbitrary", "arbitrary", "arbitrary")),
        name="flash_diff",
    )(qT, kp, vTb, lam4, subln_col)


MLA_CHAIN = 4
MLA_GRP = 8


def _flash_mla_body(qT_ref, kp_ref, vT_ref, wuvT_ref, out_ref, acc_ref, m_ref, l_ref, sa_ref, sb_ref, *, tq):
    i = pl.program_id(2)
    nch = MLA_GRP // MLA_CHAIN
    qs = [jnp.concatenate([qT_ref[c * MLA_CHAIN + rr] for rr in range(MLA_CHAIN)], axis=1) for c in range(nch)]
    o = _flash_chains(qs, [kp_ref.at[0]] * nch, [vT_ref] * nch, i, tq, acc_ref, m_ref, l_ref, sa_ref, sb_ref)
    for c in range(nch):
        oT = o[c].astype(BF16)
        for pr in range(MLA_CHAIN // 2):
            h0 = c * MLA_CHAIN + 2 * pr
            parts = [_dot(wuvT_ref[h0 + e], oT[:, (2 * pr + e) * tq:(2 * pr + e + 1) * tq]) for e in range(2)]
            blk = jnp.concatenate(parts, axis=0)
            out_ref[:, (h0 // 2) * LANES:(h0 // 2 + 1) * LANES] = blk.T.astype(out_ref.dtype)


def _flash_mla(qT, kp, ckvTb, wuvT, *, batch, tq):
    nh, dk, m = qT.shape
    t = m // batch
    nq = t // tq
    r = ckvTb.shape[1]
    return pl.pallas_call(
        functools.partial(_flash_mla_body, tq=tq),
        grid=(batch, nh // MLA_GRP, nq),
        in_specs=[pl.BlockSpec((MLA_GRP, dk, tq), lambda b, g, i: (g, 0, b * nq + i)),
                  pl.BlockSpec((1, t, dk), lambda b, g, i: (0, b, 0)),
                  pl.BlockSpec((nq, r, tq), lambda b, g, i: (b, 0, 0)),
                  pl.BlockSpec((MLA_GRP, MLA_V, r), lambda b, g, i: (g, 0, 0))],
        out_specs=pl.BlockSpec((tq, MLA_GRP * MLA_V), lambda b, g, i: (b * nq + i, g)),
        out_shape=jax.ShapeDtypeStruct((m, nh * MLA_V), BF16),
        scratch_shapes=[pltpu.VMEM((MLA_GRP // MLA_CHAIN, r, MLA_CHAIN * tq), F32),
                        pltpu.VMEM((MLA_GRP // MLA_CHAIN, 1, MLA_CHAIN * tq), F32),
                        pltpu.VMEM((MLA_GRP // MLA_CHAIN, 1, MLA_CHAIN * tq), F32),
                        pltpu.VMEM((MLA_GRP // MLA_CHAIN, tq, MLA_CHAIN * tq), F32),
                        pltpu.VMEM((MLA_GRP // MLA_CHAIN, tq, MLA_CHAIN * tq), F32)],
        compiler_params=_params(("arbitrary", "arbitrary", "arbitrary")),
        name="flash_mla",
    )(qT, kp, ckvTb, wuvT)


def _proj_mla_body(x_ref, g_ref, wdq_ref, qn_ref, wnT_ref, wrT_ref, wrsT_ref, wukbdT_ref,
                   wc_ref, kvn_ref, wkr_ref, wkrs_ref, cos_ref, sin_ref, cosT_ref, sinT_ref,
                   qT_ref, kp_ref, ckv_ref, kpeT_ref, ckvTb_ref, *, tk):
    tm = x_ref.shape[0]
    cos, sin = cos_ref[...], sin_ref[...]
    cosT, sinT = cosT_ref[...], sinT_ref[...]
    xn = _rms_rows(x_ref[...], g_ref[...], EPS).astype(BF16)
    cq = _rms_rows(_dot(xn, wdq_ref[...]), qn_ref[...], EPS).astype(BF16)
    qnT = _dot_nt(wnT_ref[...], cq).astype(BF16)
    qrT = _dot_nt(wrT_ref[...], cq)
    qrsT = _dot_nt(wrsT_ref[...], cq)
    for pr in range(wukbdT_ref.shape[0]):
        qlT = _dot(wukbdT_ref[pr], qnT[pr * LANES:(pr + 1) * LANES, :])
        for e in range(2):
            h = 2 * pr + e
            hs = slice(h * LANES, (h + 1) * LANES)
            qT_ref[h, 0:LANES, :] = qlT[e * LANES:(e + 1) * LANES, :].astype(BF16)
            qT_ref[h, LANES:2 * LANES, :] = (qrT[hs, :] * cosT + qrsT[hs, :] * sinT).astype(BF16)
    ckv = _rms_rows(_dot(xn, wc_ref[...]), kvn_ref[...], EPS)
    kr = _dot(xn, wkr_ref[...]) * cos + _dot(xn, wkrs_ref[...]) * sin
    ckv_ref[...] = ckv
    kp_ref[0, :, 0:LANES] = ckv.astype(BF16)
    kp_ref[0, :, LANES:2 * LANES] = kr.astype(BF16)
    kpeT_ref[0] = kr.T[0:MLA_ROPE, :]
    ckvT = ckv.T.astype(BF16)
    for s in range(tm // tk):
        ckvTb_ref[s] = ckvT[:, s * tk:(s + 1) * tk]


def _proj_mla(x, g, w, cos, sin, *, batch, tk):
    m, d = x.shape
    t = m // batch
    tm = min(512, t)
    tps = t // tm
    row = lambda i: (i, 0)
    r = w["wc"].shape[1]
    consts = [w["wdq"], w["qn"], w["wnT"], w["wrT"], w["wrsT"], w["wukbdT"], w["wc"], w["kvn"], w["wkr"], w["wkrs"]]
    col = lambda i: (0, i)
    return pl.pallas_call(
        functools.partial(_proj_mla_body, tk=tk),
        grid=(m // tm,),
        in_specs=[pl.BlockSpec((tm, d), row), _const_spec((1, d))] + [_const_spec(a.shape) for a in consts]
                 + [pl.BlockSpec((tm, LANES), row), pl.BlockSpec((tm, LANES), row),
                    pl.BlockSpec((LANES, tm), col), pl.BlockSpec((LANES, tm), col)],
        out_specs=[pl.BlockSpec((MLA_HEADS, 2 * LANES, tm), lambda i: (0, 0, i)),
                   pl.BlockSpec((1, tm, 2 * LANES), lambda i: (0, i, 0)),
                   pl.BlockSpec((tm, r), row),
                   pl.BlockSpec((1, MLA_ROPE, tm), lambda i: (i // tps, 0, i % tps)),
                   pl.BlockSpec((tm // tk, r, tk), lambda i: (i, 0, 0))],
        out_shape=[jax.ShapeDtypeStruct((MLA_HEADS, 2 * LANES, m), BF16),
                   jax.ShapeDtypeStruct((1, m, 2 * LANES), BF16),
                   jax.ShapeDtypeStruct((m, r), F32),
                   jax.ShapeDtypeStruct((batch, MLA_ROPE, t), F32),
                   jax.ShapeDtypeStruct((m // tk, r, tk), BF16)],
        compiler_params=_params(("arbitrary",)),
        name="proj_mla",
    )(x, g, *consts, cos, sin, cos.T, sin.T)


def _uv_body(o_ref, wuv_ref, out_ref):
    for h in range(o_ref.shape[0]):
        out_ref[h] = _dot(o_ref[h].astype(BF16), wuv_ref[h]).astype(out_ref.dtype)


def _uv_sample(o_lat, wuv):
    h, m, _ = o_lat.shape
    return pl.pallas_call(
        _uv_body,
        out_shape=jax.ShapeDtypeStruct((h, m, wuv.shape[2]), BF16),
        compiler_params=pltpu.CompilerParams(vmem_limit_bytes=VMEM_LIMIT),
        name="uv_sample",
    )(o_lat, wuv)


def _ffn_body(h_ref, a_ref, wo_ref, bo_ref, g_ref, win_ref, cw_ref, cb_ref, wout_ref, gfin_ref, prev_ref,
              out_ref, cst_ref, gs_ref, carry_ref, *, time_major, tiles_per_seq, final, fc):
    i = pl.program_id(0)
    tm = h_ref.shape[0]
    dff = wout_ref.shape[0]
    h1 = h_ref[...] + _dot(a_ref[...], wo_ref[...]) + bo_ref[...]
    xn = _rms_rows(h1, g_ref[...], EPS).astype(BF16)
    if not time_major:
        @pl.when(i % tiles_per_seq == 0)
        def _():
            carry_ref[...] = jnp.zeros(carry_ref.shape, F32)
    acc = h1
    nchunk = dff // fc

    def up(c):
        return (_dot(xn, win_ref[:, c * fc:(c + 1) * fc]),
                _dot(xn, win_ref[:, dff + c * fc:dff + (c + 1) * fc]))

    nxt = up(0)
    for c in range(nchunk):
        cs = slice(c * fc, (c + 1) * fc)
        gch, uch = nxt
        if c + 1 < nchunk:
            nxt = up(c + 1)
        cw = cw_ref[:, cs]
        if time_major:
            nb = prev_ref.shape[0] // (CONV_W - 1)
            gs = gs_ref.at[c]
            gs[0:2 * nb, :] = prev_ref[:, cs]
            gs[2 * nb:2 * nb + tm, :] = gch
            gc = cb_ref[:, cs] + cw[0:1] * gs[0:tm, :] + cw[1:2] * gs[nb:nb + tm, :] + cw[2:3] * gch
            cst_ref[:, cs] = gch[tm - 2 * nb:tm, :]
        else:
            gs = gs_ref.at[c]
            gs[0:8, :] = carry_ref[:, cs]
            gs[8:8 + tm, :] = gch
            gc = cb_ref[:, cs] + cw[0:1] * gs[6:6 + tm, :] + cw[1:2] * gs[7:7 + tm, :] + cw[2:3] * gch
            carry_ref[:, cs] = gch[tm - 8:tm, :]
        hid = (jax.nn.gelu(gc) * uch).astype(BF16)
        acc = acc + _dot(hid, wout_ref[cs, :])
    if not time_major:
        cst_ref[0] = carry_ref[...]
    if final:
        acc = _rms_rows(acc, gfin_ref[...], EPS)
    out_ref[...] = acc


def _ffn(h, a, wo, bo, g, win, cw, cb, wout, gfin, prev, *, time_major, seq, final):
    m, d = h.shape
    dff = wout.shape[0]
    fc = 256
    if time_major:
        tm, tiles_per_seq = m, 1
        cst_shape = (prev.shape[0], dff)
        cst_spec = pl.BlockSpec(cst_shape, lambda i: (0, 0))
        gs_rows = prev.shape[0] + tm
    else:
        tm = min(512, seq)
        tiles_per_seq = seq // tm
        cst_shape = (m // tm, 8, dff)
        cst_spec = pl.BlockSpec((1, 8, dff), lambda i: (i, 0, 0))
        gs_rows = 8 + tm
    row = lambda i: (i, 0)
    return pl.pallas_call(
        functools.partial(_ffn_body, time_major=time_major, tiles_per_seq=tiles_per_seq, final=final, fc=fc),
        grid=(m // tm,),
        in_specs=[pl.BlockSpec((tm, d), row), pl.BlockSpec((tm, a.shape[1]), row),
                  _const_spec(wo.shape), _const_spec(bo.shape), _const_spec(g.shape),
                  _const_spec(win.shape), _const_spec(cw.shape), _const_spec(cb.shape),
                  _const_spec(wout.shape), _const_spec(gfin.shape), _const_spec(prev.shape)],
        out_specs=[pl.BlockSpec((tm, d), row), cst_spec],
        out_shape=[jax.ShapeDtypeStruct((m, d), F32), jax.ShapeDtypeStruct(cst_shape, F32)],
        scratch_shapes=[pltpu.VMEM((dff // fc, gs_rows, fc), F32), pltpu.VMEM((8, dff), F32)],
        compiler_params=_params(("arbitrary",)),
        name="ffn_sample" if time_major else "ffn",
    )(h, a, wo, bo, g, win, cw, cb, wout, gfin, prev)


def _swa_prompt_body(qT_ref, kprev_ref, kcur_ref, vprev_ref, vcur_ref, bias_ref, sink_ref, out_ref):
    i = pl.program_id(1)
    w = kcur_ref.shape[1]
    r = SWA_R
    for g in range(SWA_G):
        q = jnp.concatenate([qT_ref[g * r + rr] for rr in range(r)], axis=1)
        kcat = jnp.concatenate([kprev_ref[g], kcur_ref[g]], axis=0)
        s = _dot(kcat, q) + bias_ref[g]
        krow = lax.broadcasted_iota(jnp.int32, s.shape, 0)
        s = jnp.where(jnp.logical_and(i == 0, krow < w), NEG, s)
        sink = sink_ref[g]
        mx = jnp.maximum(jnp.max(s, axis=0, keepdims=True), sink)
        p = jnp.exp(s - mx)
        l = jnp.sum(p, axis=0, keepdims=True) + jnp.exp(sink - mx)
        hs = slice(g * SWA_HD, (g + 1) * SWA_HD)
        vcat = jnp.concatenate([vprev_ref[hs, :], vcur_ref[hs, :]], axis=1)
        oT = _dot(vcat, p.astype(BF16)) / l
        for pr in range(r // 2):
            blk = jnp.concatenate([oT[:, (2 * pr) * w:(2 * pr + 1) * w], oT[:, (2 * pr + 1) * w:(2 * pr + 2) * w]], axis=0)
            c0 = (g * r + 2 * pr) * SWA_HD
            out_ref[:, c0:c0 + 2 * SWA_HD] = blk.T.astype(out_ref.dtype)


def _swa_prompt(qT, kp, vTb, bias, sink, *, batch):
    nh, _, m = qT.shape
    w = WINDOW
    t = m // batch
    nq = t // w
    c = SWA_G * SWA_HD
    vT2 = jnp.transpose(vTb, (1, 0, 2)).reshape(c, m)
    cur = lambda b, i: (0, b * nq + i, 0)
    prv = lambda b, i: (0, b * nq + jnp.maximum(i - 1, 0), 0)
    return pl.pallas_call(
        _swa_prompt_body,
        grid=(batch, nq),
        in_specs=[pl.BlockSpec((nh, LANES, w), lambda b, i: (0, 0, b * nq + i)),
                  pl.BlockSpec((SWA_G, w, LANES), prv), pl.BlockSpec((SWA_G, w, LANES), cur),
                  pl.BlockSpec((c, w), lambda b, i: (0, b * nq + jnp.maximum(i - 1, 0))),
                  pl.BlockSpec((c, w), lambda b, i: (0, b * nq + i)),
                  pl.BlockSpec(bias.shape, lambda b, i: (0, 0, 0)),
                  pl.BlockSpec(sink.shape, lambda b, i: (0, 0, 0))],
        out_specs=pl.BlockSpec((w, nh * SWA_HD), lambda b, i: (b * nq + i, 0)),
        out_shape=jax.ShapeDtypeStruct((m, nh * SWA_HD), BF16),
        compiler_params=_params(("arbitrary", "arbitrary")),
        name="swa_prompt",
    )(qT, kp, kp, vT2, vT2, bias, sink)


def _swa_decode_body(q_ref, kT_ref, vT_ref, kn_ref, vn_ref, bold_ref, bnew_ref, sink_ref, o_ref):
    for b in range(q_ref.shape[0]):
        q = q_ref[b]
        s_old = _dot(q, kT_ref[b].astype(BF16)) + bold_ref[...]
        s_new = _dot_nt(q, kn_ref[b].astype(BF16)) + bnew_ref[...]
        sink = sink_ref[...]
        mx = jnp.maximum(jnp.maximum(jnp.max(s_old, axis=1, keepdims=True), jnp.max(s_new, axis=1, keepdims=True)), sink)
        p_old = jnp.exp(s_old - mx)
        p_new = jnp.exp(s_new - mx)
        l = jnp.sum(p_old, axis=1, keepdims=True) + jnp.sum(p_new, axis=1, keepdims=True) + jnp.exp(sink - mx)
        o = _dot_nt(p_old.astype(BF16), vT_ref[b].astype(BF16)) + _dot(p_new.astype(BF16), vn_ref[b].astype(BF16))
        o_ref[b] = o / l


def _swa_decode(qbd, kT, vT, kn, vn, bold, bnew, sink):
    nb = qbd.shape[0]
    bb = min(8, nb)
    rows = qbd.shape[1]
    blk = lambda shp: pl.BlockSpec((bb,) + shp, lambda i: (i, 0, 0))
    cst = lambda a: pl.BlockSpec(a.shape, lambda i: (0, 0))
    return pl.pallas_call(
        _swa_decode_body,
        grid=(nb // bb,),
        in_specs=[blk(qbd.shape[1:]), blk(kT.shape[1:]), blk(vT.shape[1:]), blk(kn.shape[1:]), blk(vn.shape[1:]),
                  cst(bold), cst(bnew), cst(sink)],
        out_specs=blk((rows, kT.shape[1])),
        out_shape=jax.ShapeDtypeStruct((nb, rows, kT.shape[1]), F32),
        compiler_params=_params(("arbitrary",)),
        name="swa_decode",
    )(qbd, kT, vT, kn, vn, bold, bnew, sink)


def _page_copies(pt_ref, srcs, bufs, sem, layer, gidx, slot, nc, ch):
    b = gidx // nc
    c = gidx % nc
    out = []
    for p in range(ch):
        pg = pt_ref[b, c * ch + p]
        for src, buf in zip(srcs, bufs):
            out.append(pltpu.make_async_copy(src.at[layer, pg], buf.at[slot, p], sem.at[slot]))
    return out


def _decode_diff_body(pt_ref, kc_ref, vc_ref, q_ref, kn_ref, vn_ref, slope_ref, lam4_ref, subln_ref, o_ref,
                      kbuf, vbuf, sem, *, layer, ch, nc, nbuf, nb, past_len, lam_init):
    total = nb * nc
    rows = q_ref.shape[1]
    hg = rows // DIFF_G
    slope = slope_ref[...]
    copies = functools.partial(_page_copies, pt_ref, (kc_ref, vc_ref), (kbuf, vbuf), sem, layer, nc=nc, ch=ch)

    for ahead in range(nbuf - 1):
        for cp in copies(ahead, ahead):
            cp.start()

    def chunk(gidx, slot, c, q, carry):
        m, l, acc = carry

        @pl.when(gidx + nbuf - 1 < total)
        def _():
            for cp in copies(gidx + nbuf - 1, (slot + nbuf - 1) % nbuf):
                cp.start()

        for cp in copies(gidx, slot):
            cp.wait()
        s = jnp.concatenate([_dot(q, kbuf[slot, p].astype(BF16)) for p in range(ch)], axis=1)
        kpos = c * (ch * PAGE) + lax.broadcasted_iota(jnp.int32, (1, ch * PAGE), 1)
        s = s + slope * kpos.astype(F32)
        m_new = jnp.maximum(m, jnp.max(s, axis=1, keepdims=True))
        alpha = jnp.exp2(m - m_new)
        p = jnp.exp2(s - m_new)
        l = alpha * l + jnp.sum(p, axis=1, keepdims=True)
        pb = p.astype(BF16)
        pv = []
        for g in range(DIFF_G):
            vg = vbuf[slot, :, pl.ds(g, PAGE, stride=DIFF_G), :].reshape(ch * PAGE, LANES).astype(BF16)
            pv.append(_dot(pb[g * hg:(g + 1) * hg], vg))
        acc = alpha * acc + jnp.concatenate(pv, axis=0)
        return m_new, l, acc

    def batch_step(b, carry):
        q = q_ref[b]

        def group(cc, st):
            for slot in range(nbuf):
                c = nbuf * cc + slot
                st = chunk(b * nc + c, slot, c, q, st)
            return st

        init = (jnp.full((rows, 1), NEG, F32), jnp.zeros((rows, 1), F32), jnp.zeros((rows, LANES), F32))
        m, l, acc = lax.fori_loop(0, nc // nbuf, group, init)
        s_n = _dot_nt(q, kn_ref[b].astype(BF16))
        tn = lax.broadcasted_iota(jnp.int32, s_n.shape, 1)
        tq = lax.broadcasted_iota(jnp.int32, s_n.shape, 0) & 3
        s_n = jnp.where(tn <= tq, s_n + slope * (past_len + tn).astype(F32), NEG)
        m_new = jnp.maximum(m, jnp.max(s_n, axis=1, keepdims=True))
        alpha = jnp.exp2(m - m_new)
        p_n = jnp.exp2(s_n - m_new)
        l = alpha * l + jnp.sum(p_n, axis=1, keepdims=True)
        pnb = p_n.astype(BF16)
        pv = [_dot(pnb[g * hg:(g + 1) * hg], vn_ref[b, g].astype(BF16)) for g in range(DIFF_G)]
        o = (alpha * acc + jnp.concatenate(pv, axis=0)) / l
        lam = _diff_lambda(lam4_ref[...], lam_init)
        hh = hg // 2
        for g in range(DIFF_G):
            d = o[g * hg:g * hg + hh] - lam * o[g * hg + hh:(g + 1) * hg]
            d = _rms_rows(d, subln_ref[...], DIFF_SUBLN_EPS) * (1.0 - lam_init)
            o_ref[b, g] = d
        return carry

    lax.fori_loop(0, nb, batch_step, 0)


def _decode_diff(pt, kc, vc, qbd, kn, vn, slope, lam4, subln, *, layer, lam_init):
    nb, npages = pt.shape
    ch = min(8, npages // 2)
    nc = npages // ch
    nbuf = min(PAGE_SLOTS, nc)
    rows = qbd.shape[1]
    vm = lambda a: pl.BlockSpec(a.shape, lambda i, pt_: (0,) * a.ndim)
    return pl.pallas_call(
        functools.partial(_decode_diff_body, layer=layer, ch=ch, nc=nc, nbuf=nbuf, nb=nb, past_len=npages * PAGE, lam_init=lam_init),
        grid_spec=pltpu.PrefetchScalarGridSpec(
            num_scalar_prefetch=1, grid=(1,),
            in_specs=[pl.BlockSpec(memory_space=pl.ANY), pl.BlockSpec(memory_space=pl.ANY),
                      vm(qbd), vm(kn), vm(vn), vm(slope), vm(lam4), vm(subln)],
            out_specs=pl.BlockSpec((nb, DIFF_G, rows // 4, LANES), lambda i, pt_: (0, 0, 0, 0)),
            scratch_shapes=[pltpu.VMEM((nbuf, ch) + kc.shape[2:], F32), pltpu.VMEM((nbuf, ch) + vc.shape[2:], F32),
                            pltpu.SemaphoreType.DMA((nbuf,))]),
        out_shape=jax.ShapeDtypeStruct((nb, DIFF_G, rows // 4, LANES), F32),
        compiler_params=_params(("arbitrary",)),
        name="decode_diff",
    )(pt, kc, vc, qbd, kn, vn, slope, lam4, subln)


def _decode_mla_body(pt_ref, cc_ref, pc_ref, ql_ref, qp_ref, cn_ref, pn_ref, o_ref, cbuf, pbuf, sem,
                     *, layer, ch, nc, nbuf, nb, t_s):
    total = nb * nc
    rows = ql_ref.shape[1]
    r = cbuf.shape[-1]
    copies = functools.partial(_page_copies, pt_ref, (cc_ref, pc_ref), (cbuf, pbuf), sem, layer, nc=nc, ch=ch)

    for ahead in range(nbuf - 1):
        for cp in copies(ahead, ahead):
            cp.start()

    def chunk(gidx, slot, ql, qpe, carry):
        m, l, acc = carry

        @pl.when(gidx + nbuf - 1 < total)
        def _():
            for cp in copies(gidx + nbuf - 1, (slot + nbuf - 1) % nbuf):
                cp.start()

        for cp in copies(gidx, slot):
            cp.wait()
        ckv = cbuf[slot].reshape(ch * PAGE, r).astype(BF16)
        s = _dot_nt(ql, ckv) + jnp.concatenate([_dot(qpe, pbuf[slot, p].astype(BF16)) for p in range(ch)], axis=1)
        m_new = jnp.maximum(m, jnp.max(s, axis=1, keepdims=True))
        alpha = jnp.exp2(m - m_new)
        p = jnp.exp2(s - m_new)
        l = alpha * l + jnp.sum(p, axis=1, keepdims=True)
        acc = alpha * acc + _dot(p.astype(BF16), ckv)
        return m_new, l, acc

    def batch_step(b, carry):
        ql = ql_ref[b]
        qpe = qp_ref[b]

        def group(cc, st):
            for slot in range(nbuf):
                st = chunk(b * nc + nbuf * cc + slot, slot, ql, qpe, st)
            return st

        init = (jnp.full((rows, 1), NEG, F32), jnp.zeros((rows, 1), F32), jnp.zeros((rows, r), F32))
        m, l, acc = lax.fori_loop(0, nc // nbuf, group, init)
        cn = cn_ref[b].astype(BF16)
        s_n = _dot_nt(ql, cn) + _dot_nt(qpe, pn_ref[b].astype(BF16))
        tn = lax.broadcasted_iota(jnp.int32, s_n.shape, 1)
        tq = lax.broadcasted_iota(jnp.int32, s_n.shape, 0) // (rows // t_s)
        s_n = jnp.where(tn <= tq, s_n, NEG)
        m_new = jnp.maximum(m, jnp.max(s_n, axis=1, keepdims=True))
        alpha = jnp.exp2(m - m_new)
        p_n = jnp.exp2(s_n - m_new)
        l = alpha * l + jnp.sum(p_n, axis=1, keepdims=True)
        o_ref[b] = (alpha * acc + _dot(p_n.astype(BF16), cn)) / l
        return carry

    lax.fori_loop(0, nb, batch_step, 0)


def _decode_mla(pt, cc, pc, ql, qpe, cn, pn, *, layer, t_s):
    nb, npages = pt.shape
    ch = min(16, npages // 2)
    nc = npages // ch
    nbuf = min(PAGE_SLOTS, nc)
    rows = ql.shape[1]
    vm = lambda a: pl.BlockSpec(a.shape, lambda i, pt_: (0,) * a.ndim)
    return pl.pallas_call(
        functools.partial(_decode_mla_body, layer=layer, ch=ch, nc=nc, nbuf=nbuf, nb=nb, t_s=t_s),
        grid_spec=pltpu.PrefetchScalarGridSpec(
            num_scalar_prefetch=1, grid=(1,),
            in_specs=[pl.BlockSpec(memory_space=pl.ANY), pl.BlockSpec(memory_space=pl.ANY),
                      vm(ql), vm(qpe), vm(cn), vm(pn)],
            out_specs=pl.BlockSpec((nb, rows, cc.shape[-1]), lambda i, pt_: (0, 0, 0)),
            scratch_shapes=[pltpu.VMEM((nbuf, ch) + cc.shape[2:], F32), pltpu.VMEM((nbuf, ch) + pc.shape[2:], F32),
                            pltpu.SemaphoreType.DMA((nbuf,))]),
        out_shape=jax.ShapeDtypeStruct((nb, rows, cc.shape[-1]), F32),
        compiler_params=_params(("arbitrary",)),
        name="decode_mla",
    )(pt, cc, pc, ql, qpe, cn, pn)


def _alibi(n):
    return np.array([2.0 ** (-8.0 * (h + 1) / n) for h in range(n)], dtype=np.float32)


def _pad_heads(w, nh, hd, scale=1.0):
    d = w.shape[0]
    w3 = (w * scale).reshape(d, nh, hd)
    return jnp.pad(w3, ((0, 0), (0, 0), (0, LANES - hd))).reshape(d, nh * LANES).astype(BF16)


def _pos_aug(pos):
    hi = (pos // POS_SPLIT).astype(F32)
    lo = (pos % POS_SPLIT).astype(F32)
    cols = jnp.stack([hi, hi, lo, lo], axis=1)
    return jnp.pad(cols, ((0, 0), (DIFF_HD, LANES - DIFF_HD - 4)))


def _slope_aug(slopes):
    s = jnp.asarray(slopes, F32)
    hi = s.astype(BF16).astype(F32)
    lo = (s - hi).astype(BF16).astype(F32)
    cols = jnp.stack([hi * POS_SPLIT, lo * POS_SPLIT, hi, lo], axis=1)
    return jnp.pad(cols, ((0, 0), (DIFF_HD, LANES - DIFF_HD - 4))).reshape(1, -1)


def _rope_tables(pos):
    half = MLA_ROPE // 2
    freqs = ROPE_THETA ** (-jnp.arange(half, dtype=F32) * 2.0 / MLA_ROPE)
    ang = pos.astype(F32)[:, None] * freqs[None, :]
    cos = jnp.repeat(jnp.cos(ang), 2, axis=1)
    sin = jnp.stack([-jnp.sin(ang), jnp.sin(ang)], axis=-1).reshape(-1, MLA_ROPE)
    pad = ((0, 0), (0, LANES - MLA_ROPE))
    return jnp.pad(cos, pad), jnp.pad(sin, pad)


def _swap_pairs(w):
    return w.reshape(w.shape[:-1] + (w.shape[-1] // 2, 2))[..., ::-1].reshape(w.shape)


def _tm(x):
    return jnp.swapaxes(x, 0, 1).reshape((x.shape[0] * x.shape[1],) + x.shape[2:])


def kernel(x_prompt, x_sample, cache_diff_k, cache_diff_v, cache_mla_ckv, cache_mla_kpe, state_swa_k, state_swa_v,
           state_ffn_conv, page_table, norm_mix, norm_ffn, norm_final, diff_wq, diff_wk, diff_wv, diff_lq1, diff_lk1,
           diff_lq2, diff_lk2, diff_subln, diff_wo, mla_wdq, mla_q_norm, mla_wuq, mla_wdkv, mla_kv_norm, mla_wuk,
           mla_wuv, mla_wo, swa_wqkv, swa_bqkv, swa_sinks, swa_wo, swa_bo, ffn_w_in, ffn_conv_w, ffn_conv_b, ffn_w_out):
    bsz, seq, d = x_prompt.shape
    nb, t_s, _ = x_sample.shape
    depth = norm_mix.shape[0]
    dff = ffn_w_out.shape[1]
    npages = page_table.shape[1]
    past_len = npages * PAGE
    n_pool = cache_diff_k.shape[1]
    mp, ms = bsz * seq, nb * t_s
    tq = min(FLASH_TILE, seq)

    hp = x_prompt.reshape(mp, d)
    hs = _tm(x_sample)
    pos_p = jnp.tile(jnp.arange(seq, dtype=jnp.int32), bsz)
    pos_s = past_len + jnp.repeat(jnp.arange(t_s, dtype=jnp.int32), nb)
    zrow = lambda n: jnp.zeros((1, n), F32)

    kc = jnp.transpose(cache_diff_k, (0, 1, 3, 4, 5, 2)).reshape(cache_diff_k.shape[0], n_pool, 2 * DIFF_G * DIFF_HD, PAGE)
    vc = cache_diff_v.reshape(cache_diff_v.shape[0], n_pool, PAGE * DIFF_G, 2 * DIFF_HD)
    pc = jnp.transpose(cache_mla_kpe, (0, 1, 3, 2))

    outs = {k: [] for k in ("dkp", "dvp", "dks", "dvs", "mcp", "mpp", "mcs", "mps", "skp", "svp", "sks", "svs", "cvp", "cvs")}
    a_p = a_s = None

    for i in range(depth):
        kind, j = i % N_MIXERS, i // N_MIXERS
        gmix = norm_mix[i].reshape(1, d)
        if kind == 0:
            lam_init = 0.8 - 0.6 * math.exp(-0.3 * i)
            slopes = _alibi(DIFF_HEADS).reshape(DIFF_G, DIFF_R)
            wq = diff_wq[j].reshape(d, DIFF_G, DIFF_R, 2, DIFF_HD).transpose(0, 1, 3, 2, 4).reshape(d, -1)
            wq = _pad_heads(wq, 2 * DIFF_HEADS, DIFF_HD, DIFF_HD ** -0.5 * LOG2E)
            qcol = _slope_aug(np.broadcast_to(slopes[:, None, :], (DIFF_G, 2, DIFF_R)).reshape(-1) * LOG2E).reshape(-1, 1)
            wk2 = _pad_heads(diff_wk[j], 2 * DIFF_G, DIFF_HD)
            wkvT = jnp.concatenate([diff_wk[j], diff_wv[j]], axis=1).T.astype(BF16)
            wv = diff_wv[j].astype(BF16)
            c = wv.shape[1]
            proj_w = (wq.T, qcol, wk2, zrow(wk2.shape[1]))
            proj_w2 = (wkvT, jnp.zeros((2 * c, 1), F32), wv, zrow(c))
            lam4 = jnp.stack([diff_lq1[j], diff_lk1[j], diff_lq2[j], diff_lk2[j]])
            wo = diff_wo[j].astype(BF16)
            bo = zrow(d)

            qT, kp, kT, v, _, vTb = _proj(hp, gmix, *proj_w, _pos_aug(pos_p), *proj_w2, batch=bsz, tk=tq)
            a_p = _flash_diff(qT, kp, vTb, lam4, diff_subln[j].reshape(-1, 1), batch=bsz, tq=tq, lam_init=lam_init)
            outs["dkp"].append(jnp.transpose(kT.reshape(bsz, DIFF_G, 2, DIFF_HD, seq), (0, 4, 1, 2, 3)))
            outs["dvp"].append(v.reshape(bsz, seq, DIFF_G, 2 * DIFF_HD))

            qs, _, kTs, vs, _, _ = _proj(hs, gmix, *proj_w, _pos_aug(pos_s), *proj_w2, batch=1, tk=ms)
            k_s = kTs[0].T.reshape(t_s, nb, DIFF_G, 2, DIFF_HD)
            v_s = vs.reshape(t_s, nb, DIFF_G, 2 * DIFF_HD)
            q6 = qs[:, :DIFF_HD, :].reshape(2 * DIFF_G, DIFF_R, DIFF_HD, t_s, nb)
            eye = jnp.eye(2 * DIFF_G, dtype=BF16)
            qbd = jnp.einsum("xrdtb,xy->bxrtyd", q6, eye).reshape(nb, 2 * DIFF_G * DIFF_R * t_s, 2 * DIFF_G * DIFF_HD)
            kn = jnp.pad(jnp.transpose(k_s, (1, 0, 2, 3, 4)).reshape(nb, t_s, -1), ((0, 0), (0, 8 - t_s), (0, 0)))
            vn = jnp.pad(jnp.transpose(v_s, (1, 2, 0, 3)), ((0, 0), (0, 0), (0, 8 - t_s), (0, 0)))
            slope_rows = (np.broadcast_to(slopes[:, None, :, None], (DIFF_G, 2, DIFF_R, t_s)).reshape(-1, 1) * LOG2E).astype(np.float32)
            o_s = _decode_diff(page_table, kc, vc, qbd, kn, vn, jnp.asarray(slope_rows), lam4,
                               diff_subln[j].reshape(1, -1), layer=j, lam_init=lam_init)
            a_s = jnp.transpose(o_s.reshape(nb, DIFF_G, DIFF_R, t_s, 2 * DIFF_HD), (3, 0, 1, 2, 4)).reshape(ms, -1).astype(BF16)
            outs["dks"].append(jnp.transpose(k_s, (1, 0, 2, 3, 4)))
            outs["dvs"].append(jnp.transpose(v_s, (1, 0, 2, 3)))
        elif kind == 1:
            scale = (MLA_NOPE + MLA_ROPE) ** -0.5 * LOG2E
            hq = MLA_NOPE + MLA_ROPE
            r = mla_wdkv.shape[2] - MLA_ROPE
            wuq = mla_wuq[j].reshape(-1, MLA_HEADS, hq) * scale
            w_rope = wuq[:, :, MLA_NOPE:]
            wdkv_r = mla_wdkv[j][:, r:]
            wukT = jnp.transpose(mla_wuk[j], (1, 2, 0))
            z = jnp.zeros_like(wukT[0::2])
            wukbd = jnp.concatenate([jnp.concatenate([wukT[0::2], z], axis=2),
                                     jnp.concatenate([z, wukT[1::2]], axis=2)], axis=1).astype(BF16)
            w = dict(
                wdq=mla_wdq[j].astype(BF16), qn=mla_q_norm[j].reshape(1, -1),
                wnT=wuq[:, :, :MLA_NOPE].reshape(wuq.shape[0], -1).T.astype(BF16),
                wrT=_pad_heads(w_rope.reshape(wuq.shape[0], -1), MLA_HEADS, MLA_ROPE).T,
                wrsT=_pad_heads(_swap_pairs(w_rope).reshape(wuq.shape[0], -1), MLA_HEADS, MLA_ROPE).T,
                wukbdT=jnp.transpose(wukbd, (0, 2, 1)), wc=mla_wdkv[j][:, :r].astype(BF16), kvn=mla_kv_norm[j].reshape(1, -1),
                wkr=_pad_heads(wdkv_r, 1, MLA_ROPE), wkrs=_pad_heads(_swap_pairs(wdkv_r), 1, MLA_ROPE))
            wuvT = jnp.transpose(mla_wuv[j], (1, 2, 0)).astype(BF16)
            wo = mla_wo[j].astype(BF16)
            bo = zrow(d)

            cos_p, sin_p = _rope_tables(pos_p)
            qT, kp, ckv, kpeT, ckvTb = _proj_mla(hp, gmix, w, cos_p, sin_p, batch=bsz, tk=tq)
            a_p = _flash_mla(qT, kp, ckvTb, wuvT, batch=bsz, tq=tq)
            outs["mcp"].append(ckv.reshape(bsz, seq, r))
            outs["mpp"].append(jnp.transpose(kpeT, (0, 2, 1)))

            cos_s, sin_s = _rope_tables(pos_s)
            qs, _, ckv_s, kpeT_s, _ = _proj_mla(hs, gmix, w, cos_s, sin_s, batch=1, tk=ms)
            kpe_s = kpeT_s[0].T.reshape(t_s, nb, MLA_ROPE)
            ckv_s = ckv_s.reshape(t_s, nb, r)
            q4 = jnp.transpose(qs.reshape(MLA_HEADS, 2 * LANES, t_s, nb), (3, 2, 0, 1)).reshape(nb, t_s * MLA_HEADS, 2 * LANES)
            pad8 = ((0, 0), (0, 8 - t_s), (0, 0))
            o_s = _decode_mla(page_table, cache_mla_ckv, pc, q4[:, :, :r], q4[:, :, LANES:LANES + MLA_ROPE],
                              jnp.pad(jnp.transpose(ckv_s, (1, 0, 2)), pad8), jnp.pad(jnp.transpose(kpe_s, (1, 0, 2)), pad8),
                              layer=j, t_s=t_s)
            o_h = jnp.transpose(o_s.reshape(nb, t_s, MLA_HEADS, r), (2, 1, 0, 3)).reshape(MLA_HEADS, ms, r)
            a_s = jnp.transpose(_uv_sample(o_h, jnp.transpose(wuvT, (0, 2, 1))), (1, 0, 2)).reshape(ms, -1)
            outs["mcs"].append(jnp.transpose(ckv_s, (1, 0, 2)))
            outs["mps"].append(jnp.transpose(kpe_s, (1, 0, 2)))
        else:
            nq_, nk_ = SWA_HEADS * SWA_HD, SWA_G * SWA_HD
            slopes = _alibi(SWA_HEADS).reshape(SWA_G, SWA_R)
            wqkv, bqkv = swa_wqkv[j], swa_bqkv[j]
            sc = SWA_HD ** -0.5
            wq = _pad_heads(wqkv[:, :nq_], SWA_HEADS, SWA_HD, sc)
            qcol = jnp.pad((bqkv[:nq_] * sc).reshape(SWA_HEADS, SWA_HD), ((0, 0), (0, LANES - SWA_HD))).reshape(-1, 1)
            wk2 = _pad_heads(wqkv[:, nq_:nq_ + nk_], SWA_G, SWA_HD)
            krow = jnp.pad(bqkv[nq_:nq_ + nk_].reshape(SWA_G, SWA_HD), ((0, 0), (0, LANES - SWA_HD))).reshape(1, -1)
            proj_w = (wq.T, qcol, wk2, krow)
            proj_w2 = (wqkv[:, nq_:].T.astype(BF16), bqkv[nq_:].reshape(-1, 1), wqkv[:, nq_ + nk_:].astype(BF16),
                       bqkv[nq_ + nk_:].reshape(1, -1))
            wo = swa_wo[j].astype(BF16)
            bo = swa_bo[j].reshape(1, d)
            sinks = swa_sinks[j].reshape(SWA_G, SWA_R)
            w_ = WINDOW

            dist = (w_ + np.arange(w_)[None, :] - np.arange(2 * w_)[:, None]).astype(np.float32)
            valid = (dist >= 0) & (dist < w_)
            bias = np.where(valid[None, None], -slopes[:, :, None, None] * dist[None, None], NEG)
            bias = np.transpose(bias, (0, 2, 1, 3)).reshape(SWA_G, 2 * w_, SWA_R * w_).astype(np.float32)
            sink_p = jnp.broadcast_to(sinks[:, :, None], (SWA_G, SWA_R, w_)).reshape(SWA_G, 1, SWA_R * w_)
            kaug0 = jnp.zeros((mp, LANES), F32)
            qT, kp, kT, _, vT, vTb = _proj(hp, gmix, *proj_w, kaug0, *proj_w2, batch=bsz, tk=w_)
            a_p = _swa_prompt(qT, kp, vTb, jnp.asarray(bias), sink_p, batch=bsz)
            tail = lambda xT: jnp.transpose(xT[:, :, seq - w_:].reshape(bsz, SWA_G, SWA_HD, w_), (0, 3, 1, 2))
            outs["skp"].append(tail(kT))
            outs["svp"].append(tail(vT))

            qs, _, kTs, _, vTs, _ = _proj(hs, gmix, *proj_w, jnp.zeros((ms, LANES), F32), *proj_w2, batch=1, tk=ms)
            k_s = kTs[0].T.reshape(t_s, nb, SWA_G, SWA_HD)
            v_s = vTs[0].T.reshape(t_s, nb, SWA_G, SWA_HD)
            q5 = qs[:, :SWA_HD, :].reshape(SWA_G, SWA_R, SWA_HD, t_s, nb)
            qbd = jnp.einsum("grdtb,gy->bgrtyd", q5, jnp.eye(SWA_G, dtype=BF16)).reshape(nb, SWA_HEADS * t_s, nk_)
            stT = lambda st: jnp.transpose(st, (0, 2, 3, 1)).reshape(nb, nk_, w_)
            pad8 = ((0, 0), (0, 8 - t_s), (0, 0))
            kn = jnp.pad(jnp.transpose(k_s, (1, 0, 2, 3)).reshape(nb, t_s, nk_), pad8)
            vn = jnp.pad(jnp.transpose(v_s, (1, 0, 2, 3)).reshape(nb, t_s, nk_), pad8)
            tt = np.arange(t_s)
            d_old = (tt[:, None] + w_ - np.arange(w_)[None, :]).astype(np.float32)
            d_new = (tt[:, None] - np.arange(8)[None, :]).astype(np.float32)
            mk = lambda dd, ok: np.where(ok[None, None], -slopes[:, :, None, None] * dd[None, None], NEG).reshape(
                SWA_HEADS * t_s, -1).astype(np.float32)
            bold = mk(d_old, d_old < w_)
            bnew = mk(d_new, (d_new >= 0) & (np.arange(8)[None, :] < t_s))
            sink_s = jnp.broadcast_to(sinks[:, :, None], (SWA_G, SWA_R, t_s)).reshape(-1, 1)
            o_s = _swa_decode(qbd, stT(state_swa_k[j]), stT(state_swa_v[j]), kn, vn, jnp.asarray(bold), jnp.asarray(bnew), sink_s)
            o5 = o_s.reshape(nb, SWA_G, SWA_R, t_s, SWA_G, SWA_HD)
            o_g = jnp.stack([o5[:, g, :, :, g] for g in range(SWA_G)], axis=1)
            a_s = jnp.transpose(o_g, (3, 0, 1, 2, 4)).reshape(ms, -1).astype(BF16)
            k_bt = jnp.transpose(k_s, (1, 0, 2, 3))
            v_bt = jnp.transpose(v_s, (1, 0, 2, 3))
            outs["sks"].append(jnp.concatenate([state_swa_k[j], k_bt], axis=1)[:, -w_:])
            outs["svs"].append(jnp.concatenate([state_swa_v[j], v_bt], axis=1)[:, -w_:])

        final = i == depth - 1
        gffn = norm_ffn[i].reshape(1, d)
        gfin = norm_final.reshape(1, d)
        win = ffn_w_in[i].astype(BF16)
        wout = ffn_w_out[i].astype(BF16)
        cw = ffn_conv_w[i]
        cb = ffn_conv_b[i].reshape(1, dff)
        hp, cst = _ffn(hp, a_p, wo, bo, gffn, win, cw, cb, wout, gfin, jnp.zeros((8, LANES), F32),
                       time_major=False, seq=seq, final=final)
        tps = cst.shape[0] // bsz
        outs["cvp"].append(cst.reshape(bsz, tps, 8, dff)[:, -1, 8 - (CONV_W - 1):])
        prev = jnp.swapaxes(state_ffn_conv[i], 0, 1).reshape((CONV_W - 1) * nb, dff)
        hs, cst_s = _ffn(hs, a_s, wo, bo, gffn, win, cw, cb, wout, gfin, prev, time_major=True, seq=t_s, final=final)
        outs["cvs"].append(jnp.swapaxes(cst_s.reshape(CONV_W - 1, nb, dff), 0, 1))

    y_prompt = hp.reshape(bsz, seq, d)
    y_sample = jnp.swapaxes(hs.reshape(t_s, nb, d), 0, 1)
    st = lambda k: jnp.stack(outs[k])
    return (y_prompt, y_sample, st("dkp"), st("dvp"), st("dks"), st("dvs"), st("mcp"), st("mpp"), st("mcs"), st("mps"),
            st("skp"), st("svp"), st("sks"), st("svs"), st("cvp"), st("cvs"))
```

```python
import functools
import math

import numpy as np
import jax
import jax.numpy as jnp
from jax import lax
from jax.experimental import pallas as pl
from jax.experimental.pallas import tpu as pltpu

F32 = jnp.float32
BF16 = jnp.bfloat16
NEG = -1e30
LANES = 128
VMEM_LIMIT = 56 * 1024 * 1024

N_MIXERS = 3
DIFF_HEADS, DIFF_G, DIFF_HD = 8, 2, 64
DIFF_R = DIFF_HEADS // DIFF_G
DIFF_SUBLN_EPS = 1e-5
MLA_HEADS, MLA_NOPE, MLA_ROPE, MLA_V = 16, 64, 32, 64
ROPE_THETA = 10000.0
SWA_HEADS, SWA_G, SWA_HD, WINDOW = 16, 2, 64, 128
SWA_R = SWA_HEADS // SWA_G
CONV_W = 3
PAGE = 128
EPS = 1e-6
POS_SPLIT = 64
FLASH_TILE = 512
PAGE_SLOTS = 4
LOG2E = math.log2(math.e)


def _dot(a, b):
    return jnp.dot(a, b, preferred_element_type=F32)


def _dot_nt(a, b):
    return lax.dot_general(a, b, (((1,), (1,)), ((), ())), preferred_element_type=F32)


def _rms_rows(x, g, eps):
    return x * lax.rsqrt(jnp.mean(x * x, axis=-1, keepdims=True) + eps) * g


def _params(sem):
    return pltpu.CompilerParams(dimension_semantics=sem, vmem_limit_bytes=VMEM_LIMIT)


def _const_spec(shape):
    nd = len(shape)
    return pl.BlockSpec(shape, lambda *_: (0,) * nd, pipeline_mode=pl.Buffered(1))


def _proj_body(x_ref, g_ref, wqT_ref, qcol_ref, qaug_ref, wk_ref, krow_ref, kaug_ref, wkvT_ref, kvcol_ref, wv_ref, vrow_ref,
               qT_ref, kp_ref, kT_ref, v_ref, vT32_ref, vTb_ref, *, tk):
    tm = x_ref.shape[0]
    c = kT_ref.shape[1]
    nq = qT_ref.shape[0]
    hd = wqT_ref.shape[0] // nq
    xn = _rms_rows(x_ref[...], g_ref[...], EPS).astype(BF16)
    qT = _dot_nt(wqT_ref[...], xn) + qcol_ref[...]
    na = LANES - hd
    for h in range(nq):
        qT_ref[h, 0:hd, :] = qT[h * hd:(h + 1) * hd, :].astype(BF16)
        qT_ref[h, hd:LANES, :] = jnp.broadcast_to(qaug_ref[h * na:(h + 1) * na, :], (na, tm)).astype(BF16)
    k2 = _dot(xn, wk_ref[...]) + krow_ref[...]
    kaug = kaug_ref[...]
    for j in range(kp_ref.shape[0]):
        kp_ref[j] = (k2[:, j * LANES:(j + 1) * LANES] + kaug).astype(BF16)
    kvT = _dot_nt(wkvT_ref[...], xn) + kvcol_ref[...]
    kT_ref[0] = kvT[:c]
    vT = kvT[c:]
    vT32_ref[0] = vT
    vTb = vT.astype(BF16)
    for s in range(tm // tk):
        vTb_ref[s] = vTb[:, s * tk:(s + 1) * tk]
    v_ref[...] = _dot(xn, wv_ref[...]) + vrow_ref[...]


def _proj(x, g, wqT, qcol, qaug, wk, krow, kaug, wkvT, kvcol, wv, vrow, *, batch, tk, nq):
    m, d = x.shape
    t = m // batch
    tm = min(512, t)
    nk, c = wk.shape[1] // LANES, wv.shape[1]
    tps = t // tm
    row = lambda i: (i, 0)
    consts = (wqT, qcol, qaug, wk, krow)
    consts2 = (wkvT, kvcol, wv, vrow)
    return pl.pallas_call(
        functools.partial(_proj_body, tk=tk),
        grid=(m // tm,),
        in_specs=[pl.BlockSpec((tm, d), row), _const_spec((1, d))] + [_const_spec(a.shape) for a in consts]
                 + [pl.BlockSpec((tm, LANES), row)] + [_const_spec(a.shape) for a in consts2],
        out_specs=[pl.BlockSpec((nq, LANES, tm), lambda i: (0, 0, i)),
                   pl.BlockSpec((nk, tm, LANES), lambda i: (0, i, 0)),
                   pl.BlockSpec((1, c, tm), lambda i: (i // tps, 0, i % tps)),
                   pl.BlockSpec((tm, c), row),
                   pl.BlockSpec((1, c, tm), lambda i: (i // tps, 0, i % tps)),
                   pl.BlockSpec((tm // tk, c, tk), lambda i: (i, 0, 0))],
        out_shape=[jax.ShapeDtypeStruct((nq, LANES, m), BF16),
                   jax.ShapeDtypeStruct((nk, m, LANES), BF16),
                   jax.ShapeDtypeStruct((batch, c, t), F32),
                   jax.ShapeDtypeStruct((m, c), F32),
                   jax.ShapeDtypeStruct((batch, c, t), F32),
                   jax.ShapeDtypeStruct((m // tk, c, tk), BF16)],
        compiler_params=_params(("arbitrary",)),
        name="proj",
    )(x, g, *consts, kaug, *consts2)


def _flash_chains(qs, k_refs, vT_refs, i, tq, acc_ref, m_ref, l_ref, sa_ref, sb_ref):
    n = len(qs)

    def qk(t, s_ref):
        dt = pl.ds(pl.multiple_of(t * tq, tq), tq)
        for c in range(n):
            s_ref[c] = _dot(k_refs[c][dt, :], qs[c])

    def update(t, s_ref, masked):
        for c in range(n):
            s = s_ref[c]
            if masked:
                krow = lax.broadcasted_iota(jnp.int32, s.shape, 0)
                qcol = lax.broadcasted_iota(jnp.int32, s.shape, 1) & (tq - 1)
                s = jnp.where(krow <= qcol, s, NEG)
            m_old = m_ref[c]
            m_new = jnp.maximum(m_old, jnp.max(s, axis=0, keepdims=True))
            alpha = jnp.exp2(m_old - m_new)
            p = jnp.exp2(s - m_new)
            l_ref[c] = alpha * l_ref[c] + jnp.sum(p, axis=0, keepdims=True)
            acc_ref[c] = alpha * acc_ref[c] + _dot(vT_refs[c][t], p.astype(BF16))
            m_ref[c] = m_new

    m_ref[...] = jnp.full(m_ref.shape, NEG, F32)
    l_ref[...] = jnp.zeros(l_ref.shape, F32)
    acc_ref[...] = jnp.zeros(acc_ref.shape, F32)
    qk(0, sa_ref)

    def pair(jj, carry):
        t = 2 * jj
        qk(t + 1, sb_ref)
        update(t, sa_ref, False)
        qk(t + 2, sa_ref)
        update(t + 1, sb_ref, False)
        return carry

    lax.fori_loop(0, i // 2, pair, 0)

    @pl.when(i % 2 == 0)
    def _():
        update(i, sa_ref, True)

    @pl.when(i % 2 == 1)
    def _():
        qk(i, sb_ref)
        update(i - 1, sa_ref, False)
        update(i, sb_ref, True)

    return [acc_ref[c] / l_ref[c] for c in range(n)]


def _diff_lambda(lam4, lam_init):
    a = jnp.sum(lam4[0:1] * lam4[1:2], axis=-1, keepdims=True)
    b = jnp.sum(lam4[2:3] * lam4[3:4], axis=-1, keepdims=True)
    return jnp.exp(a) - jnp.exp(b) + lam_init


def _flash_diff_body(qT_ref, kp_ref, vT_ref, lam4_ref, subln_ref, out_ref, acc_ref, m_ref, l_ref, sa_ref, sb_ref,
                     *, tq, lam_init):
    i = pl.program_id(2)
    r = DIFF_R
    qs = [jnp.concatenate([qT_ref[mm * r + rr] for rr in range(r)], axis=1) for mm in range(2)]
    o = _flash_chains(qs, [kp_ref.at[0], kp_ref.at[1]], [vT_ref, vT_ref], i, tq, acc_ref, m_ref, l_ref, sa_ref, sb_ref)
    lam = _diff_lambda(lam4_ref[...], lam_init)
    od = o[0] - lam * o[1]
    ms = jnp.mean(od * od, axis=0, keepdims=True)
    od = od * lax.rsqrt(ms + DIFF_SUBLN_EPS) * subln_ref[...] * (1.0 - lam_init)
    for rr in range(r):
        out_ref[:, rr * LANES:(rr + 1) * LANES] = od[:, rr * tq:(rr + 1) * tq].T.astype(out_ref.dtype)


def _flash_diff(qT, kp, vTb, lam4, subln_col, *, batch, tq, lam_init):
    nh, _, m = qT.shape
    t = m // batch
    nq = t // tq
    r = DIFF_R
    return pl.pallas_call(
        functools.partial(_flash_diff_body, tq=tq, lam_init=lam_init),
        grid=(batch, DIFF_G, nq),
        in_specs=[pl.BlockSpec((2 * r, LANES, tq), lambda b, g, i: (g, 0, b * nq + i)),
                  pl.BlockSpec((2, t, LANES), lambda b, g, i: (g, b, 0)),
                  pl.BlockSpec((nq, 2 * DIFF_HD, tq), lambda b, g, i: (b, g, 0)),
                  pl.BlockSpec(lam4.shape, lambda b, g, i: (0, 0)),
                  pl.BlockSpec(subln_col.shape, lambda b, g, i: (0, 0))],
        out_specs=pl.BlockSpec((tq, r * LANES), lambda b, g, i: (b * nq + i, g)),
        out_shape=jax.ShapeDtypeStruct((m, DIFF_G * r * LANES), BF16),
        scratch_shapes=[pltpu.VMEM((2, 2 * DIFF_HD, r * tq), F32),
                        pltpu.VMEM((2, 1, r * tq), F32), pltpu.VMEM((2, 1, r * tq), F32),
                        pltpu.VMEM((2, tq, r * tq), F32), pltpu.VMEM((2, tq, r * tq), F32)],
        compiler_params=_params(("arbitrary", "arbitrary", "arbitrary")),
        name="flash_diff",
    )(qT, kp, vTb, lam4, subln_col)


MLA_CHAIN = 4
MLA_GRP = 8


def _flash_mla_body(qT_ref, kp_ref, vT_ref, wuvT_ref, out_ref, acc_ref, m_ref, l_ref, sa_ref, sb_ref, *, tq):
    i = pl.program_id(2)
    nch = MLA_GRP // MLA_CHAIN
    qs = [jnp.concatenate([qT_ref[c * MLA_CHAIN + rr] for rr in range(MLA_CHAIN)], axis=1) for c in range(nch)]
    o = _flash_chains(qs, [kp_ref.at[0]] * nch, [vT_ref] * nch, i, tq, acc_ref, m_ref, l_ref, sa_ref, sb_ref)
    for c in range(nch):
        oT = o[c].astype(BF16)
        for pr in range(MLA_CHAIN // 2):
            h0 = c * MLA_CHAIN + 2 * pr
            parts = [_dot(wuvT_ref[h0 + e], oT[:, (2 * pr + e) * tq:(2 * pr + e + 1) * tq]) for e in range(2)]
            blk = jnp.concatenate(parts, axis=0)
            out_ref[:, (h0 // 2) * LANES:(h0 // 2 + 1) * LANES] = blk.T.astype(out_ref.dtype)


def _flash_mla(qT, kp, ckvTb, wuvT, *, batch, tq):
    nh, dk, m = qT.shape
    t = m // batch
    nq = t // tq
    r = ckvTb.shape[1]
    return pl.pallas_call(
        functools.partial(_flash_mla_body, tq=tq),
        grid=(batch, nh // MLA_GRP, nq),
        in_specs=[pl.BlockSpec((MLA_GRP, dk, tq), lambda b, g, i: (g, 0, b * nq + i)),
                  pl.BlockSpec((1, t, dk), lambda b, g, i: (0, b, 0)),
                  pl.BlockSpec((nq, r, tq), lambda b, g, i: (b, 0, 0)),
                  pl.BlockSpec((MLA_GRP, MLA_V, r), lambda b, g, i: (g, 0, 0))],
        out_specs=pl.BlockSpec((tq, MLA_GRP * MLA_V), lambda b, g, i: (b * nq + i, g)),
        out_shape=jax.ShapeDtypeStruct((m, nh * MLA_V), BF16),
        scratch_shapes=[pltpu.VMEM((MLA_GRP // MLA_CHAIN, r, MLA_CHAIN * tq), F32),
                        pltpu.VMEM((MLA_GRP // MLA_CHAIN, 1, MLA_CHAIN * tq), F32),
                        pltpu.VMEM((MLA_GRP // MLA_CHAIN, 1, MLA_CHAIN * tq), F32),
                        pltpu.VMEM((MLA_GRP // MLA_CHAIN, tq, MLA_CHAIN * tq), F32),
                        pltpu.VMEM((MLA_GRP // MLA_CHAIN, tq, MLA_CHAIN * tq), F32)],
        compiler_params=_params(("arbitrary", "arbitrary", "arbitrary")),
        name="flash_mla",
    )(qT, kp, ckvTb, wuvT)


def _proj_mla_body(x_ref, g_ref, wdq_ref, qn_ref, wnT_ref, wrT_ref, wrsT_ref, wukbdT_ref,
                   wc_ref, kvn_ref, wkr_ref, wkrs_ref, cos_ref, sin_ref, cosT_ref, sinT_ref,
                   qT_ref, kp_ref, ckv_ref, kpeT_ref, ckvTb_ref, *, tk):
    tm = x_ref.shape[0]
    cos, sin = cos_ref[...], sin_ref[...]
    cosT, sinT = cosT_ref[...], sinT_ref[...]
    xn = _rms_rows(x_ref[...], g_ref[...], EPS).astype(BF16)
    cq = _rms_rows(_dot(xn, wdq_ref[...]), qn_ref[...], EPS).astype(BF16)
    qnT = _dot_nt(wnT_ref[...], cq).astype(BF16)
    qrT = _dot_nt(wrT_ref[...], cq)
    qrsT = _dot_nt(wrsT_ref[...], cq)
    rp = cosT.shape[0]
    zpad = jnp.zeros((LANES - rp, tm), BF16)
    for pr in range(wukbdT_ref.shape[0]):
        qlT = _dot(wukbdT_ref[pr], qnT[pr * LANES:(pr + 1) * LANES, :])
        for e in range(2):
            h = 2 * pr + e
            hs = slice(h * rp, (h + 1) * rp)
            qT_ref[h, 0:LANES, :] = qlT[e * LANES:(e + 1) * LANES, :].astype(BF16)
            qT_ref[h, LANES:LANES + rp, :] = (qrT[hs, :] * cosT + qrsT[hs, :] * sinT).astype(BF16)
            qT_ref[h, LANES + rp:2 * LANES, :] = zpad
    ckv = _rms_rows(_dot(xn, wc_ref[...]), kvn_ref[...], EPS)
    kr = _dot(xn, wkr_ref[...]) * cos + _dot(xn, wkrs_ref[...]) * sin
    ckv_ref[...] = ckv
    kp_ref[0, :, 0:LANES] = ckv.astype(BF16)
    kp_ref[0, :, LANES:2 * LANES] = kr.astype(BF16)
    kpeT_ref[0] = kr.T[0:MLA_ROPE, :]
    ckvT = ckv.T.astype(BF16)
    for s in range(tm // tk):
        ckvTb_ref[s] = ckvT[:, s * tk:(s + 1) * tk]


def _proj_mla(x, g, w, cos, sin, *, batch, tk):
    m, d = x.shape
    t = m // batch
    tm = min(512, t)
    tps = t // tm
    row = lambda i: (i, 0)
    r = w["wc"].shape[1]
    consts = [w["wdq"], w["qn"], w["wnT"], w["wrT"], w["wrsT"], w["wukbdT"], w["wc"], w["kvn"], w["wkr"], w["wkrs"]]
    col = lambda i: (0, i)
    return pl.pallas_call(
        functools.partial(_proj_mla_body, tk=tk),
        grid=(m // tm,),
        in_specs=[pl.BlockSpec((tm, d), row), _const_spec((1, d))] + [_const_spec(a.shape) for a in consts]
                 + [pl.BlockSpec((tm, LANES), row), pl.BlockSpec((tm, LANES), row),
                    pl.BlockSpec((MLA_ROPE, tm), col), pl.BlockSpec((MLA_ROPE, tm), col)],
        out_specs=[pl.BlockSpec((MLA_HEADS, 2 * LANES, tm), lambda i: (0, 0, i)),
                   pl.BlockSpec((1, tm, 2 * LANES), lambda i: (0, i, 0)),
                   pl.BlockSpec((tm, r), row),
                   pl.BlockSpec((1, MLA_ROPE, tm), lambda i: (i // tps, 0, i % tps)),
                   pl.BlockSpec((tm // tk, r, tk), lambda i: (i, 0, 0))],
        out_shape=[jax.ShapeDtypeStruct((MLA_HEADS, 2 * LANES, m), BF16),
                   jax.ShapeDtypeStruct((1, m, 2 * LANES), BF16),
                   jax.ShapeDtypeStruct((m, r), F32),
                   jax.ShapeDtypeStruct((batch, MLA_ROPE, t), F32),
                   jax.ShapeDtypeStruct((m // tk, r, tk), BF16)],
        compiler_params=_params(("arbitrary",)),
        name="proj_mla",
    )(x, g, *consts, cos, sin, cos[:, :MLA_ROPE].T, sin[:, :MLA_ROPE].T)


def _uv_body(o_ref, wuv_ref, out_ref):
    for h in range(o_ref.shape[0]):
        out_ref[h] = _dot(o_ref[h].astype(BF16), wuv_ref[h]).astype(out_ref.dtype)


def _uv_sample(o_lat, wuv):
    h, m, _ = o_lat.shape
    return pl.pallas_call(
        _uv_body,
        out_shape=jax.ShapeDtypeStruct((h, m, wuv.shape[2]), BF16),
        compiler_params=pltpu.CompilerParams(vmem_limit_bytes=VMEM_LIMIT),
        name="uv_sample",
    )(o_lat, wuv)


def _ffn_body(h_ref, a_ref, wo_ref, bo_ref, g_ref, win_ref, cw_ref, cb_ref, wout_ref, gfin_ref, prev_ref,
              out_ref, cst_ref, gs_ref, carry_ref, *, time_major, tiles_per_seq, final, fc):
    i = pl.program_id(0)
    tm = h_ref.shape[0]
    dff = wout_ref.shape[0]
    h1 = h_ref[...] + _dot(a_ref[...], wo_ref[...]) + bo_ref[...]
    xn = _rms_rows(h1, g_ref[...], EPS).astype(BF16)
    if not time_major:
        @pl.when(i % tiles_per_seq == 0)
        def _():
            carry_ref[...] = jnp.zeros(carry_ref.shape, F32)
    acc = h1
    nchunk = dff // fc

    def up(c):
        return (_dot(xn, win_ref[:, c * fc:(c + 1) * fc]),
                _dot(xn, win_ref[:, dff + c * fc:dff + (c + 1) * fc]))

    nxt = up(0)
    for c in range(nchunk):
        cs = slice(c * fc, (c + 1) * fc)
        gch, uch = nxt
        if c + 1 < nchunk:
            nxt = up(c + 1)
        cw = cw_ref[:, cs]
        if time_major:
            nb = prev_ref.shape[0] // (CONV_W - 1)
            gs = gs_ref.at[c % 2]
            gs[0:2 * nb, :] = prev_ref[:, cs]
            gs[2 * nb:2 * nb + tm, :] = gch
            gc = cb_ref[:, cs] + cw[0:1] * gs[0:tm, :] + cw[1:2] * gs[nb:nb + tm, :] + cw[2:3] * gch
            cst_ref[:, cs] = gch[tm - 2 * nb:tm, :]
        else:
            gs = gs_ref.at[c % 2]
            gs[0:8, :] = carry_ref[:, cs]
            gs[8:8 + tm, :] = gch
            gc = cb_ref[:, cs] + cw[0:1] * gs[6:6 + tm, :] + cw[1:2] * gs[7:7 + tm, :] + cw[2:3] * gch
            carry_ref[:, cs] = gch[tm - 8:tm, :]
        hid = (jax.nn.gelu(gc) * uch).astype(BF16)
        acc = acc + _dot(hid, wout_ref[cs, :])
    if not time_major:
        cst_ref[0] = carry_ref[...]
    if final:
        acc = _rms_rows(acc, gfin_ref[...], EPS)
    out_ref[...] = acc


def _ffn(h, a, wo, bo, g, win, cw, cb, wout, gfin, prev, *, time_major, seq, final):
    m, d = h.shape
    dff = wout.shape[0]
    fc = 256
    if time_major:
        tm, tiles_per_seq = m, 1
        cst_shape = (prev.shape[0], dff)
        cst_spec = pl.BlockSpec(cst_shape, lambda i: (0, 0))
        gs_rows = prev.shape[0] + tm
    else:
        tm = min(512, seq)
        tiles_per_seq = seq // tm
        cst_shape = (m // tm, 8, dff)
        cst_spec = pl.BlockSpec((1, 8, dff), lambda i: (i, 0, 0))
        gs_rows = 8 + tm
    row = lambda i: (i, 0)
    return pl.pallas_call(
        functools.partial(_ffn_body, time_major=time_major, tiles_per_seq=tiles_per_seq, final=final, fc=fc),
        grid=(m // tm,),
        in_specs=[pl.BlockSpec((tm, d), row), pl.BlockSpec((tm, a.shape[1]), row),
                  _const_spec(wo.shape), _const_spec(bo.shape), _const_spec(g.shape),
                  _const_spec(win.shape), _const_spec(cw.shape), _const_spec(cb.shape),
                  _const_spec(wout.shape), _const_spec(gfin.shape), _const_spec(prev.shape)],
        out_specs=[pl.BlockSpec((tm, d), row), cst_spec],
        out_shape=[jax.ShapeDtypeStruct((m, d), F32), jax.ShapeDtypeStruct(cst_shape, F32)],
        scratch_shapes=[pltpu.VMEM((2, gs_rows, fc), F32), pltpu.VMEM((8, dff), F32)],
        compiler_params=_params(("arbitrary",)),
        name="ffn_sample" if time_major else "ffn",
    )(h, a, wo, bo, g, win, cw, cb, wout, gfin, prev)


def _swa_prompt_body(qT_ref, kprev_ref, kcur_ref, vprev_ref, vcur_ref, bias_ref, sink_ref, out_ref):
    i = pl.program_id(1)
    w = kcur_ref.shape[1]
    r = SWA_R
    for g in range(SWA_G):
        q = jnp.concatenate([qT_ref[g * r + rr] for rr in range(r)], axis=1)
        kcat = jnp.concatenate([kprev_ref[g], kcur_ref[g]], axis=0)
        s = _dot(kcat, q) + bias_ref[g]
        krow = lax.broadcasted_iota(jnp.int32, s.shape, 0)
        s = jnp.where(jnp.logical_and(i == 0, krow < w), NEG, s)
        sink = sink_ref[g]
        mx = jnp.maximum(jnp.max(s, axis=0, keepdims=True), sink)
        p = jnp.exp(s - mx)
        l = jnp.sum(p, axis=0, keepdims=True) + jnp.exp(sink - mx)
        hs = slice(g * SWA_HD, (g + 1) * SWA_HD)
        vcat = jnp.concatenate([vprev_ref[hs, :], vcur_ref[hs, :]], axis=1)
        oT = _dot(vcat, p.astype(BF16)) / l
        for pr in range(r // 2):
            blk = jnp.concatenate([oT[:, (2 * pr) * w:(2 * pr + 1) * w], oT[:, (2 * pr + 1) * w:(2 * pr + 2) * w]], axis=0)
            c0 = (g * r + 2 * pr) * SWA_HD
            out_ref[:, c0:c0 + 2 * SWA_HD] = blk.T.astype(out_ref.dtype)


def _swa_prompt(qT, kp, vTb, bias, sink, *, batch):
    nh, _, m = qT.shape
    w = WINDOW
    t = m // batch
    nq = t // w
    c = SWA_G * SWA_HD
    vT2 = jnp.transpose(vTb, (1, 0, 2)).reshape(c, m)
    cur = lambda b, i: (0, b * nq + i, 0)
    prv = lambda b, i: (0, b * nq + jnp.maximum(i - 1, 0), 0)
    return pl.pallas_call(
        _swa_prompt_body,
        grid=(batch, nq),
        in_specs=[pl.BlockSpec((nh, LANES, w), lambda b, i: (0, 0, b * nq + i)),
                  pl.BlockSpec((SWA_G, w, LANES), prv), pl.BlockSpec((SWA_G, w, LANES), cur),
                  pl.BlockSpec((c, w), lambda b, i: (0, b * nq + jnp.maximum(i - 1, 0))),
                  pl.BlockSpec((c, w), lambda b, i: (0, b * nq + i)),
                  pl.BlockSpec(bias.shape, lambda b, i: (0, 0, 0)),
                  pl.BlockSpec(sink.shape, lambda b, i: (0, 0, 0))],
        out_specs=pl.BlockSpec((w, nh * SWA_HD), lambda b, i: (b * nq + i, 0)),
        out_shape=jax.ShapeDtypeStruct((m, nh * SWA_HD), BF16),
        compiler_params=_params(("arbitrary", "arbitrary")),
        name="swa_prompt",
    )(qT, kp, kp, vT2, vT2, bias, sink)


def _swa_decode_body(q_ref, kT_ref, vT_ref, kn_ref, vn_ref, bold_ref, bnew_ref, sink_ref, o_ref):
    for b in range(q_ref.shape[0]):
        q = q_ref[b]
        s_old = _dot(q, kT_ref[b].astype(BF16)) + bold_ref[...]
        s_new = _dot_nt(q, kn_ref[b].astype(BF16)) + bnew_ref[...]
        sink = sink_ref[...]
        mx = jnp.maximum(jnp.maximum(jnp.max(s_old, axis=1, keepdims=True), jnp.max(s_new, axis=1, keepdims=True)), sink)
        p_old = jnp.exp(s_old - mx)
        p_new = jnp.exp(s_new - mx)
        l = jnp.sum(p_old, axis=1, keepdims=True) + jnp.sum(p_new, axis=1, keepdims=True) + jnp.exp(sink - mx)
        o = _dot_nt(p_old.astype(BF16), vT_ref[b].astype(BF16)) + _dot(p_new.astype(BF16), vn_ref[b].astype(BF16))
        o_ref[b] = o / l


def _swa_decode(qbd, kT, vT, kn, vn, bold, bnew, sink):
    nb = qbd.shape[0]
    bb = min(8, nb)
    rows = qbd.shape[1]
    blk = lambda shp: pl.BlockSpec((bb,) + shp, lambda i: (i, 0, 0))
    cst = lambda a: pl.BlockSpec(a.shape, lambda i: (0, 0))
    return pl.pallas_call(
        _swa_decode_body,
        grid=(nb // bb,),
        in_specs=[blk(qbd.shape[1:]), blk(kT.shape[1:]), blk(vT.shape[1:]), blk(kn.shape[1:]), blk(vn.shape[1:]),
                  cst(bold), cst(bnew), cst(sink)],
        out_specs=blk((rows, kT.shape[1])),
        out_shape=jax.ShapeDtypeStruct((nb, rows, kT.shape[1]), F32),
        compiler_params=_params(("arbitrary",)),
        name="swa_decode",
    )(qbd, kT, vT, kn, vn, bold, bnew, sink)


def _page_copies(pt_ref, srcs, bufs, sem, layer, gidx, slot, nc, ch):
    b = gidx // nc
    c = gidx % nc
    out = []
    for p in range(ch):
        pg = pt_ref[b, c * ch + p]
        for src, buf in zip(srcs, bufs):
            out.append(pltpu.make_async_copy(src.at[layer, pg], buf.at[slot, p], sem.at[slot]))
    return out


def _decode_diff_body(pt_ref, kc_ref, vc_ref, q_ref, kn_ref, vn_ref, slope_ref, lam4_ref, subln_ref, o_ref,
                      kbuf, vbuf, sem, *, layer, ch, nc, nbuf, nb, past_len, lam_init):
    total = nb * nc
    rows = q_ref.shape[1]
    hg = rows // DIFF_G
    slope = slope_ref[...]
    copies = functools.partial(_page_copies, pt_ref, (kc_ref, vc_ref), (kbuf, vbuf), sem, layer, nc=nc, ch=ch)

    for ahead in range(nbuf - 1):
        for cp in copies(ahead, ahead):
            cp.start()

    def chunk(gidx, slot, c, q, carry):
        m, l, acc = carry

        @pl.when(gidx + nbuf - 1 < total)
        def _():
            for cp in copies(gidx + nbuf - 1, (slot + nbuf - 1) % nbuf):
                cp.start()

        for cp in copies(gidx, slot):
            cp.wait()
        s = jnp.concatenate([_dot(q, kbuf[slot, p].astype(BF16)) for p in range(ch)], axis=1)
        kpos = c * (ch * PAGE) + lax.broadcasted_iota(jnp.int32, (1, ch * PAGE), 1)
        s = s + slope * kpos.astype(F32)
        m_new = jnp.maximum(m, jnp.max(s, axis=1, keepdims=True))
        alpha = jnp.exp2(m - m_new)
        p = jnp.exp2(s - m_new)
        l = alpha * l + jnp.sum(p, axis=1, keepdims=True)
        pb = p.astype(BF16)
        pv = []
        for g in range(DIFF_G):
            vg = vbuf[slot, :, pl.ds(g, PAGE, stride=DIFF_G), :].reshape(ch * PAGE, LANES).astype(BF16)
            pv.append(_dot(pb[g * hg:(g + 1) * hg], vg))
        acc = alpha * acc + jnp.concatenate(pv, axis=0)
        return m_new, l, acc

    def batch_step(b, carry):
        q = q_ref[b]

        def group(cc, st):
            for slot in range(nbuf):
                c = nbuf * cc + slot
                st = chunk(b * nc + c, slot, c, q, st)
            return st

        init = (jnp.full((rows, 1), NEG, F32), jnp.zeros((rows, 1), F32), jnp.zeros((rows, LANES), F32))
        m, l, acc = lax.fori_loop(0, nc // nbuf, group, init)
        s_n = _dot_nt(q, kn_ref[b].astype(BF16))
        tn = lax.broadcasted_iota(jnp.int32, s_n.shape, 1)
        tq = lax.broadcasted_iota(jnp.int32, s_n.shape, 0) & 3
        s_n = jnp.where(tn <= tq, s_n + slope * (past_len + tn).astype(F32), NEG)
        m_new = jnp.maximum(m, jnp.max(s_n, axis=1, keepdims=True))
        alpha = jnp.exp2(m - m_new)
        p_n = jnp.exp2(s_n - m_new)
        l = alpha * l + jnp.sum(p_n, axis=1, keepdims=True)
        pnb = p_n.astype(BF16)
        pv = [_dot(pnb[g * hg:(g + 1) * hg], vn_ref[b, g].astype(BF16)) for g in range(DIFF_G)]
        o = (alpha * acc + jnp.concatenate(pv, axis=0)) / l
        lam = _diff_lambda(lam4_ref[...], lam_init)
        hh = hg // 2
        for g in range(DIFF_G):
            d = o[g * hg:g * hg + hh] - lam * o[g * hg + hh:(g + 1) * hg]
            d = _rms_rows(d, subln_ref[...], DIFF_SUBLN_EPS) * (1.0 - lam_init)
            o_ref[b, g] = d
        return carry

    lax.fori_loop(0, nb, batch_step, 0)


def _decode_diff(pt, kc, vc, qbd, kn, vn, slope, lam4, subln, *, layer, lam_init):
    nb, npages = pt.shape
    ch = min(8, npages // 2)
    nc = npages // ch
    nbuf = min(PAGE_SLOTS, nc)
    rows = qbd.shape[1]
    vm = lambda a: pl.BlockSpec(a.shape, lambda i, pt_: (0,) * a.ndim)
    return pl.pallas_call(
        functools.partial(_decode_diff_body, layer=layer, ch=ch, nc=nc, nbuf=nbuf, nb=nb, past_len=npages * PAGE, lam_init=lam_init),
        grid_spec=pltpu.PrefetchScalarGridSpec(
            num_scalar_prefetch=1, grid=(1,),
            in_specs=[pl.BlockSpec(memory_space=pl.ANY), pl.BlockSpec(memory_space=pl.ANY),
                      vm(qbd), vm(kn), vm(vn), vm(slope), vm(lam4), vm(subln)],
            out_specs=pl.BlockSpec((nb, DIFF_G, rows // 4, LANES), lambda i, pt_: (0, 0, 0, 0)),
            scratch_shapes=[pltpu.VMEM((nbuf, ch) + kc.shape[2:], F32), pltpu.VMEM((nbuf, ch) + vc.shape[2:], F32),
                            pltpu.SemaphoreType.DMA((nbuf,))]),
        out_shape=jax.ShapeDtypeStruct((nb, DIFF_G, rows // 4, LANES), F32),
        compiler_params=_params(("arbitrary",)),
        name="decode_diff",
    )(pt, kc, vc, qbd, kn, vn, slope, lam4, subln)


def _decode_mla_body(pt_ref, cc_ref, pc_ref, ql_ref, qp_ref, cn_ref, pn_ref, o_ref, cbuf, pbuf, sem,
                     *, layer, ch, nc, nbuf, nb, t_s):
    total = nb * nc
    rows = ql_ref.shape[1]
    r = cbuf.shape[-1]
    copies = functools.partial(_page_copies, pt_ref, (cc_ref, pc_ref), (cbuf, pbuf), sem, layer, nc=nc, ch=ch)

    for ahead in range(nbuf - 1):
        for cp in copies(ahead, ahead):
            cp.start()

    def chunk(gidx, slot, ql, qpe, carry):
        m, l, acc = carry

        @pl.when(gidx + nbuf - 1 < total)
        def _():
            for cp in copies(gidx + nbuf - 1, (slot + nbuf - 1) % nbuf):
                cp.start()

        for cp in copies(gidx, slot):
            cp.wait()
        ckv = cbuf[slot].reshape(ch * PAGE, r).astype(BF16)
        s = _dot_nt(ql, ckv) + jnp.concatenate([_dot(qpe, pbuf[slot, p].astype(BF16)) for p in range(ch)], axis=1)
        m_new = jnp.maximum(m, jnp.max(s, axis=1, keepdims=True))
        alpha = jnp.exp2(m - m_new)
        p = jnp.exp2(s - m_new)
        l = alpha * l + jnp.sum(p, axis=1, keepdims=True)
        acc = alpha * acc + _dot(p.astype(BF16), ckv)
        return m_new, l, acc

    def batch_step(b, carry):
        ql = ql_ref[b]
        qpe = qp_ref[b]

        def group(cc, st):
            for slot in range(nbuf):
                st = chunk(b * nc + nbuf * cc + slot, slot, ql, qpe, st)
            return st

        init = (jnp.full((rows, 1), NEG, F32), jnp.zeros((rows, 1), F32), jnp.zeros((rows, r), F32))
        m, l, acc = lax.fori_loop(0, nc // nbuf, group, init)
        cn = cn_ref[b].astype(BF16)
        s_n = _dot_nt(ql, cn) + _dot_nt(qpe, pn_ref[b].astype(BF16))
        tn = lax.broadcasted_iota(jnp.int32, s_n.shape, 1)
        tq = lax.broadcasted_iota(jnp.int32, s_n.shape, 0) // (rows // t_s)
        s_n = jnp.where(tn <= tq, s_n, NEG)
        m_new = jnp.maximum(m, jnp.max(s_n, axis=1, keepdims=True))
        alpha = jnp.exp2(m - m_new)
        p_n = jnp.exp2(s_n - m_new)
        l = alpha * l + jnp.sum(p_n, axis=1, keepdims=True)
        o_ref[b] = (alpha * acc + _dot(p_n.astype(BF16), cn)) / l
        return carry

    lax.fori_loop(0, nb, batch_step, 0)


def _decode_mla(pt, cc, pc, ql, qpe, cn, pn, *, layer, t_s):
    nb, npages = pt.shape
    ch = min(16, npages // 2)
    nc = npages // ch
    nbuf = min(PAGE_SLOTS, nc)
    rows = ql.shape[1]
    vm = lambda a: pl.BlockSpec(a.shape, lambda i, pt_: (0,) * a.ndim)
    return pl.pallas_call(
        functools.partial(_decode_mla_body, layer=layer, ch=ch, nc=nc, nbuf=nbuf, nb=nb, t_s=t_s),
        grid_spec=pltpu.PrefetchScalarGridSpec(
            num_scalar_prefetch=1, grid=(1,),
            in_specs=[pl.BlockSpec(memory_space=pl.ANY), pl.BlockSpec(memory_space=pl.ANY),
                      vm(ql), vm(qpe), vm(cn), vm(pn)],
            out_specs=pl.BlockSpec((nb, rows, cc.shape[-1]), lambda i, pt_: (0, 0, 0)),
            scratch_shapes=[pltpu.VMEM((nbuf, ch) + cc.shape[2:], F32), pltpu.VMEM((nbuf, ch) + pc.shape[2:], F32),
                            pltpu.SemaphoreType.DMA((nbuf,))]),
        out_shape=jax.ShapeDtypeStruct((nb, rows, cc.shape[-1]), F32),
        compiler_params=_params(("arbitrary",)),
        name="decode_mla",
    )(pt, cc, pc, ql, qpe, cn, pn)


def _alibi(n):
    return np.array([2.0 ** (-8.0 * (h + 1) / n) for h in range(n)], dtype=np.float32)


def _pad_heads(w, nh, hd, scale=1.0):
    d = w.shape[0]
    w3 = (w * scale).reshape(d, nh, hd)
    return jnp.pad(w3, ((0, 0), (0, 0), (0, LANES - hd))).reshape(d, nh * LANES).astype(BF16)


def _pos_aug(pos):
    hi = (pos // POS_SPLIT).astype(F32)
    lo = (pos % POS_SPLIT).astype(F32)
    cols = jnp.stack([hi, hi, lo, lo], axis=1)
    return jnp.pad(cols, ((0, 0), (DIFF_HD, LANES - DIFF_HD - 4)))


def _slope_aug(slopes):
    s = jnp.asarray(slopes, F32)
    hi = s.astype(BF16).astype(F32)
    lo = (s - hi).astype(BF16).astype(F32)
    cols = jnp.stack([hi * POS_SPLIT, lo * POS_SPLIT, hi, lo], axis=1)
    return jnp.pad(cols, ((0, 0), (0, LANES - DIFF_HD - 4))).reshape(-1, 1)


def _rope_tables(pos):
    half = MLA_ROPE // 2
    freqs = ROPE_THETA ** (-jnp.arange(half, dtype=F32) * 2.0 / MLA_ROPE)
    ang = pos.astype(F32)[:, None] * freqs[None, :]
    cos = jnp.repeat(jnp.cos(ang), 2, axis=1)
    sin = jnp.stack([-jnp.sin(ang), jnp.sin(ang)], axis=-1).reshape(-1, MLA_ROPE)
    pad = ((0, 0), (0, LANES - MLA_ROPE))
    return jnp.pad(cos, pad), jnp.pad(sin, pad)


def _swap_pairs(w):
    return w.reshape(w.shape[:-1] + (w.shape[-1] // 2, 2))[..., ::-1].reshape(w.shape)


def _tm(x):
    return jnp.swapaxes(x, 0, 1).reshape((x.shape[0] * x.shape[1],) + x.shape[2:])


def kernel(x_prompt, x_sample, cache_diff_k, cache_diff_v, cache_mla_ckv, cache_mla_kpe, state_swa_k, state_swa_v,
           state_ffn_conv, page_table, norm_mix, norm_ffn, norm_final, diff_wq, diff_wk, diff_wv, diff_lq1, diff_lk1,
           diff_lq2, diff_lk2, diff_subln, diff_wo, mla_wdq, mla_q_norm, mla_wuq, mla_wdkv, mla_kv_norm, mla_wuk,
           mla_wuv, mla_wo, swa_wqkv, swa_bqkv, swa_sinks, swa_wo, swa_bo, ffn_w_in, ffn_conv_w, ffn_conv_b, ffn_w_out):
    bsz, seq, d = x_prompt.shape
    nb, t_s, _ = x_sample.shape
    depth = norm_mix.shape[0]
    dff = ffn_w_out.shape[1]
    npages = page_table.shape[1]
    past_len = npages * PAGE
    n_pool = cache_diff_k.shape[1]
    mp, ms = bsz * seq, nb * t_s
    tq = min(FLASH_TILE, seq)

    hp = x_prompt.reshape(mp, d)
    hs = _tm(x_sample)
    pos_p = jnp.tile(jnp.arange(seq, dtype=jnp.int32), bsz)
    pos_s = past_len + jnp.repeat(jnp.arange(t_s, dtype=jnp.int32), nb)
    zrow = lambda n: jnp.zeros((1, n), F32)

    kc = jnp.transpose(cache_diff_k, (0, 1, 3, 4, 5, 2)).reshape(cache_diff_k.shape[0], n_pool, 2 * DIFF_G * DIFF_HD, PAGE)
    vc = cache_diff_v.reshape(cache_diff_v.shape[0], n_pool, PAGE * DIFF_G, 2 * DIFF_HD)
    pc = jnp.transpose(cache_mla_kpe, (0, 1, 3, 2))

    outs = {k: [] for k in ("dkp", "dvp", "dks", "dvs", "mcp", "mpp", "mcs", "mps", "skp", "svp", "sks", "svs", "cvp", "cvs")}
    a_p = a_s = None

    for i in range(depth):
        kind, j = i % N_MIXERS, i // N_MIXERS
        gmix = norm_mix[i].reshape(1, d)
        if kind == 0:
            lam_init = 0.8 - 0.6 * math.exp(-0.3 * i)
            slopes = _alibi(DIFF_HEADS).reshape(DIFF_G, DIFF_R)
            wq = diff_wq[j].reshape(d, DIFF_G, DIFF_R, 2, DIFF_HD).transpose(0, 1, 3, 2, 4).reshape(d, -1)
            wqT = (wq * (DIFF_HD ** -0.5 * LOG2E)).T.astype(BF16)
            qaug = _slope_aug(np.broadcast_to(slopes[:, None, :], (DIFF_G, 2, DIFF_R)).reshape(-1) * LOG2E)
            wk2 = _pad_heads(diff_wk[j], 2 * DIFF_G, DIFF_HD)
            wkvT = jnp.concatenate([diff_wk[j], diff_wv[j]], axis=1).T.astype(BF16)
            wv = diff_wv[j].astype(BF16)
            c = wv.shape[1]
            proj_w = (wqT, jnp.zeros((wqT.shape[0], 1), F32), qaug, wk2, zrow(wk2.shape[1]))
            proj_w2 = (wkvT, jnp.zeros((2 * c, 1), F32), wv, zrow(c))
            nq_heads = 2 * DIFF_HEADS
            lam4 = jnp.stack([diff_lq1[j], diff_lk1[j], diff_lq2[j], diff_lk2[j]])
            wo = diff_wo[j].astype(BF16)
            bo = zrow(d)

            qT, kp, kT, v, _, vTb = _proj(hp, gmix, *proj_w, _pos_aug(pos_p), *proj_w2, batch=bsz, tk=tq, nq=nq_heads)
            a_p = _flash_diff(qT, kp, vTb, lam4, diff_subln[j].reshape(-1, 1), batch=bsz, tq=tq, lam_init=lam_init)
            outs["dkp"].append(jnp.transpose(kT.reshape(bsz, DIFF_G, 2, DIFF_HD, seq), (0, 4, 1, 2, 3)))
            outs["dvp"].append(v.reshape(bsz, seq, DIFF_G, 2 * DIFF_HD))

            qs, _, kTs, vs, _, _ = _proj(hs, gmix, *proj_w, _pos_aug(pos_s), *proj_w2, batch=1, tk=ms, nq=nq_heads)
            k_s = kTs[0].T.reshape(t_s, nb, DIFF_G, 2, DIFF_HD)
            v_s = vs.reshape(t_s, nb, DIFF_G, 2 * DIFF_HD)
            q6 = qs[:, :DIFF_HD, :].reshape(2 * DIFF_G, DIFF_R, DIFF_HD, t_s, nb)
            eye = jnp.eye(2 * DIFF_G, dtype=BF16)
            qbd = jnp.einsum("xrdtb,xy->bxrtyd", q6, eye).reshape(nb, 2 * DIFF_G * DIFF_R * t_s, 2 * DIFF_G * DIFF_HD)
            kn = jnp.pad(jnp.transpose(k_s, (1, 0, 2, 3, 4)).reshape(nb, t_s, -1), ((0, 0), (0, 8 - t_s), (0, 0)))
            vn = jnp.pad(jnp.transpose(v_s, (1, 2, 0, 3)), ((0, 0), (0, 0), (0, 8 - t_s), (0, 0)))
            slope_rows = (np.broadcast_to(slopes[:, None, :, None], (DIFF_G, 2, DIFF_R, t_s)).reshape(-1, 1) * LOG2E).astype(np.float32)
            o_s = _decode_diff(page_table, kc, vc, qbd, kn, vn, jnp.asarray(slope_rows), lam4,
                               diff_subln[j].reshape(1, -1), layer=j, lam_init=lam_init)
            a_s = jnp.transpose(o_s.reshape(nb, DIFF_G, DIFF_R, t_s, 2 * DIFF_HD), (3, 0, 1, 2, 4)).reshape(ms, -1).astype(BF16)
            outs["dks"].append(jnp.transpose(k_s, (1, 0, 2, 3, 4)))
            outs["dvs"].append(jnp.transpose(v_s, (1, 0, 2, 3)))
        elif kind == 1:
            scale = (MLA_NOPE + MLA_ROPE) ** -0.5 * LOG2E
            hq = MLA_NOPE + MLA_ROPE
            r = mla_wdkv.shape[2] - MLA_ROPE
            wuq = mla_wuq[j].reshape(-1, MLA_HEADS, hq) * scale
            w_rope = wuq[:, :, MLA_NOPE:]
            wdkv_r = mla_wdkv[j][:, r:]
            wukT = jnp.transpose(mla_wuk[j], (1, 2, 0))
            z = jnp.zeros_like(wukT[0::2])
            wukbd = jnp.concatenate([jnp.concatenate([wukT[0::2], z], axis=2),
                                     jnp.concatenate([z, wukT[1::2]], axis=2)], axis=1).astype(BF16)
            w = dict(
                wdq=mla_wdq[j].astype(BF16), qn=mla_q_norm[j].reshape(1, -1),
                wnT=wuq[:, :, :MLA_NOPE].reshape(wuq.shape[0], -1).T.astype(BF16),
                wrT=w_rope.reshape(wuq.shape[0], -1).T.astype(BF16),
                wrsT=_swap_pairs(w_rope).reshape(wuq.shape[0], -1).T.astype(BF16),
                wukbdT=jnp.transpose(wukbd, (0, 2, 1)), wc=mla_wdkv[j][:, :r].astype(BF16), kvn=mla_kv_norm[j].reshape(1, -1),
                wkr=_pad_heads(wdkv_r, 1, MLA_ROPE), wkrs=_pad_heads(_swap_pairs(wdkv_r), 1, MLA_ROPE))
            wuvT = jnp.transpose(mla_wuv[j], (1, 2, 0)).astype(BF16)
            wo = mla_wo[j].astype(BF16)
            bo = zrow(d)

            cos_p, sin_p = _rope_tables(pos_p)
            qT, kp, ckv, kpeT, ckvTb = _proj_mla(hp, gmix, w, cos_p, sin_p, batch=bsz, tk=tq)
            a_p = _flash_mla(qT, kp, ckvTb, wuvT, batch=bsz, tq=tq)
            outs["mcp"].append(ckv.reshape(bsz, seq, r))
            outs["mpp"].append(jnp.transpose(kpeT, (0, 2, 1)))

            cos_s, sin_s = _rope_tables(pos_s)
            qs, _, ckv_s, kpeT_s, _ = _proj_mla(hs, gmix, w, cos_s, sin_s, batch=1, tk=ms)
            kpe_s = kpeT_s[0].T.reshape(t_s, nb, MLA_ROPE)
            ckv_s = ckv_s.reshape(t_s, nb, r)
            q4 = jnp.transpose(qs.reshape(MLA_HEADS, 2 * LANES, t_s, nb), (3, 2, 0, 1)).reshape(nb, t_s * MLA_HEADS, 2 * LANES)
            pad8 = ((0, 0), (0, 8 - t_s), (0, 0))
            o_s = _decode_mla(page_table, cache_mla_ckv, pc, q4[:, :, :r], q4[:, :, LANES:LANES + MLA_ROPE],
                              jnp.pad(jnp.transpose(ckv_s, (1, 0, 2)), pad8), jnp.pad(jnp.transpose(kpe_s, (1, 0, 2)), pad8),
                              layer=j, t_s=t_s)
            o_h = jnp.transpose(o_s.reshape(nb, t_s, MLA_HEADS, r), (2, 1, 0, 3)).reshape(MLA_HEADS, ms, r)
            a_s = jnp.transpose(_uv_sample(o_h, jnp.transpose(wuvT, (0, 2, 1))), (1, 0, 2)).reshape(ms, -1)
            outs["mcs"].append(jnp.transpose(ckv_s, (1, 0, 2)))
            outs["mps"].append(jnp.transpose(kpe_s, (1, 0, 2)))
        else:
            nq_, nk_ = SWA_HEADS * SWA_HD, SWA_G * SWA_HD
            slopes = _alibi(SWA_HEADS).reshape(SWA_G, SWA_R)
            wqkv, bqkv = swa_wqkv[j], swa_bqkv[j]
            sc = SWA_HD ** -0.5
            wqT = (wqkv[:, :nq_] * sc).T.astype(BF16)
            qcol = (bqkv[:nq_] * sc).reshape(-1, 1)
            wk2 = _pad_heads(wqkv[:, nq_:nq_ + nk_], SWA_G, SWA_HD)
            krow = jnp.pad(bqkv[nq_:nq_ + nk_].reshape(SWA_G, SWA_HD), ((0, 0), (0, LANES - SWA_HD))).reshape(1, -1)
            proj_w = (wqT, qcol, jnp.zeros((SWA_HEADS * (LANES - SWA_HD), 1), F32), wk2, krow)
            proj_w2 = (wqkv[:, nq_:].T.astype(BF16), bqkv[nq_:].reshape(-1, 1), wqkv[:, nq_ + nk_:].astype(BF16),
                       bqkv[nq_ + nk_:].reshape(1, -1))
            wo = swa_wo[j].astype(BF16)
            bo = swa_bo[j].reshape(1, d)
            sinks = swa_sinks[j].reshape(SWA_G, SWA_R)
            w_ = WINDOW

            dist = (w_ + np.arange(w_)[None, :] - np.arange(2 * w_)[:, None]).astype(np.float32)
            valid = (dist >= 0) & (dist < w_)
            bias = np.where(valid[None, None], -slopes[:, :, None, None] * dist[None, None], NEG)
            bias = np.transpose(bias, (0, 2, 1, 3)).reshape(SWA_G, 2 * w_, SWA_R * w_).astype(np.float32)
            sink_p = jnp.broadcast_to(sinks[:, :, None], (SWA_G, SWA_R, w_)).reshape(SWA_G, 1, SWA_R * w_)
            kaug0 = jnp.zeros((mp, LANES), F32)
            qT, kp, kT, _, vT, vTb = _proj(hp, gmix, *proj_w, kaug0, *proj_w2, batch=bsz, tk=w_, nq=SWA_HEADS)
            a_p = _swa_prompt(qT, kp, vTb, jnp.asarray(bias), sink_p, batch=bsz)
            tail = lambda xT: jnp.transpose(xT[:, :, seq - w_:].reshape(bsz, SWA_G, SWA_HD, w_), (0, 3, 1, 2))
            outs["skp"].append(tail(kT))
            outs["svp"].append(tail(vT))

            qs, _, kTs, _, vTs, _ = _proj(hs, gmix, *proj_w, jnp.zeros((ms, LANES), F32), *proj_w2, batch=1, tk=ms,
                                          nq=SWA_HEADS)
            k_s = kTs[0].T.reshape(t_s, nb, SWA_G, SWA_HD)
            v_s = vTs[0].T.reshape(t_s, nb, SWA_G, SWA_HD)
            q5 = qs[:, :SWA_HD, :].reshape(SWA_G, SWA_R, SWA_HD, t_s, nb)
            qbd = jnp.einsum("grdtb,gy->bgrtyd", q5, jnp.eye(SWA_G, dtype=BF16)).reshape(nb, SWA_HEADS * t_s, nk_)
            stT = lambda st: jnp.transpose(st, (0, 2, 3, 1)).reshape(nb, nk_, w_)
            pad8 = ((0, 0), (0, 8 - t_s), (0, 0))
            kn = jnp.pad(jnp.transpose(k_s, (1, 0, 2, 3)).reshape(nb, t_s, nk_), pad8)
            vn = jnp.pad(jnp.transpose(v_s, (1, 0, 2, 3)).reshape(nb, t_s, nk_), pad8)
            tt = np.arange(t_s)
            d_old = (tt[:, None] + w_ - np.arange(w_)[None, :]).astype(np.float32)
            d_new = (tt[:, None] - np.arange(8)[None, :]).astype(np.float32)
            mk = lambda dd, ok: np.where(ok[None, None], -slopes[:, :, None, None] * dd[None, None], NEG).reshape(
                SWA_HEADS * t_s, -1).astype(np.float32)
            bold = mk(d_old, d_old < w_)
            bnew = mk(d_new, (d_new >= 0) & (np.arange(8)[None, :] < t_s))
            sink_s = jnp.broadcast_to(sinks[:, :, None], (SWA_G, SWA_R, t_s)).reshape(-1, 1)
            o_s = _swa_decode(qbd, stT(state_swa_k[j]), stT(state_swa_v[j]), kn, vn, jnp.asarray(bold), jnp.asarray(bnew), sink_s)
            o5 = o_s.reshape(nb, SWA_G, SWA_R, t_s, SWA_G, SWA_HD)
            o_g = jnp.stack([o5[:, g, :, :, g] for g in range(SWA_G)], axis=1)
            a_s = jnp.transpose(o_g, (3, 0, 1, 2, 4)).reshape(ms, -1).astype(BF16)
            k_bt = jnp.transpose(k_s, (1, 0, 2, 3))
            v_bt = jnp.transpose(v_s, (1, 0, 2, 3))
            outs["sks"].append(jnp.concatenate([state_swa_k[j], k_bt], axis=1)[:, -w_:])
            outs["svs"].append(jnp.concatenate([state_swa_v[j], v_bt], axis=1)[:, -w_:])

        final = i == depth - 1
        gffn = norm_ffn[i].reshape(1, d)
        gfin = norm_final.reshape(1, d)
        win = ffn_w_in[i].astype(BF16)
        wout = ffn_w_out[i].astype(BF16)
        cw = ffn_conv_w[i]
        cb = ffn_conv_b[i].reshape(1, dff)
        hp, cst = _ffn(hp, a_p, wo, bo, gffn, win, cw, cb, wout, gfin, jnp.zeros((8, LANES), F32),
                       time_major=False, seq=seq, final=final)
        tps = cst.shape[0] // bsz
        outs["cvp"].append(cst.reshape(bsz, tps, 8, dff)[:, -1, 8 - (CONV_W - 1):])
        prev = jnp.swapaxes(state_ffn_conv[i], 0, 1).reshape((CONV_W - 1) * nb, dff)
        hs, cst_s = _ffn(hs, a_s, wo, bo, gffn, win, cw, cb, wout, gfin, prev, time_major=True, seq=t_s, final=final)
        outs["cvs"].append(jnp.swapaxes(cst_s.reshape(CONV_W - 1, nb, dff), 0, 1))

    y_prompt = hp.reshape(bsz, seq, d)
    y_sample = jnp.swapaxes(hs.reshape(t_s, nb, d), 0, 1)
    st = lambda k: jnp.stack(outs[k])
    return (y_prompt, y_sample, st("dkp"), st("dvp"), st("dks"), st("dvs"), st("mcp"), st("mpp"), st("mcs"), st("mps"),
            st("skp"), st("svp"), st("sks"), st("svs"), st("cvp"), st("cvs"))
```
